```python
import math
import jax, jax.numpy as jnp
from jax import lax
import numpy as np

D_MODEL = 2048
BATCH = 4
SEQ = 4096
DEPTH = 2

N_HEADS = 32
N_KV_HEADS = 4
HEAD_DIM = D_MODEL // N_HEADS
GROUP = N_HEADS // N_KV_HEADS
Q_DIM = N_HEADS * HEAD_DIM
KV_DIM = N_KV_HEADS * HEAD_DIM
D_FF = 4 * D_MODEL
WINDOW = 128
MOBA_BLOCK = 256
MOBA_TOPK = 3
MOBA_Q_CHUNK = 16
N_A_LAYERS = (DEPTH + 1) // 2
N_B_LAYERS = DEPTH - N_A_LAYERS
RMS_EPS = 1e-6
NEG_INF = -1e30

kernel_name = "yoco_swa_sink_moba_hybrid"


def rmsnorm(x, g):
    xf = x.astype(jnp.float32)
    y = xf * lax.rsqrt(jnp.mean(xf * xf, axis=-1, keepdims=True) + RMS_EPS)
    return (y * g.astype(jnp.float32)).astype(x.dtype)


def alibi_slopes(n):
    return jnp.exp2(-8.0 * jnp.arange(1, n + 1, dtype=jnp.float32) / n)


def sq_relu_mlp(x, w_up, w_down):
    h = jax.nn.relu(x @ w_up)
    return (h * h) @ w_down


def sliding_window_sink_attention(q, k, v, sinks, slopes):
    B, T, H, dh = q.shape
    nb = T // WINDOW
    qb = q.reshape(B, nb, WINDOW, N_KV_HEADS, GROUP, dh)
    kb = k.reshape(B, nb, WINDOW, N_KV_HEADS, dh)
    vb = v.reshape(B, nb, WINDOW, N_KV_HEADS, dh)

    def with_prev(a):
        prev = jnp.concatenate([jnp.zeros_like(a[:, :1]), a[:, :-1]], axis=1)
        return jnp.concatenate([prev, a], axis=2)

    kk, vv = with_prev(kb), with_prev(vb)
    s = jnp.einsum('bnqkgd,bnskd->bkgnqs', qb, kk).astype(jnp.float32) * (1.0 / math.sqrt(dh))
    qi = jnp.arange(WINDOW)[:, None] + WINDOW
    si = jnp.arange(2 * WINDOW)[None, :]
    dist = qi - si
    band = (dist >= 0) & (dist < WINDOW)
    n_idx = jnp.arange(nb)[:, None, None]
    mask = band[None] & ((n_idx > 0) | (si[None] >= WINDOW))
    bias = -slopes[:, :, None, None, None] * dist.astype(jnp.float32)[None, None, None]
    s = jnp.where(mask, s + bias, NEG_INF)
    sink = jnp.broadcast_to(sinks.astype(jnp.float32).reshape(N_KV_HEADS, GROUP, 1, 1, 1),
                            s.shape[:-1] + (1,))
    p = jax.nn.softmax(jnp.concatenate([s, sink], axis=-1), axis=-1)[..., :-1]
    o = jnp.einsum('bkgnqs,bnskd->bnqkgd', p.astype(v.dtype), vv)
    return o.reshape(B, T, H * dh)


def moba_shared_kv_side(x, kv_norm, w_kv_shared):
    B, T, _ = x.shape
    kv = rmsnorm(x, kv_norm) @ w_kv_shared
    k, v = jnp.split(kv, [KV_DIM], axis=-1)
    nblk = -(-T // MOBA_BLOCK)
    pad = nblk * MOBA_BLOCK - T

    def blocks(a):
        a = jnp.pad(a.reshape(B, T, N_KV_HEADS, HEAD_DIM), ((0, 0), (0, pad), (0, 0), (0, 0)))
        return a.reshape(B, nblk, MOBA_BLOCK, N_KV_HEADS, HEAD_DIM).transpose(0, 3, 1, 2, 4)

    kblk, vblk = blocks(k), blocks(v)
    counts = jnp.clip(T - jnp.arange(nblk) * MOBA_BLOCK, 1, MOBA_BLOCK).astype(jnp.float32)
    kmean = (kblk.astype(jnp.float32).sum(axis=3) / counts[:, None]).astype(k.dtype)
    return kblk, vblk, kmean


def moba_attention(q, kblk, vblk, kmean, slopes):
    B, T, H, dh = q.shape
    nblk = kblk.shape[2]
    ks = min(MOBA_TOPK, nblk)
    scale = 1.0 / math.sqrt(dh)
    q5 = q.reshape(B, T, N_KV_HEADS, GROUP, dh)
    gate = jnp.einsum('btkgd,bknd->bkgtn', q5, kmean).astype(jnp.float32)
    qblock = jnp.arange(T) // MOBA_BLOCK
    past = jnp.arange(nblk)[None, :] < qblock[:, None]
    gate = jnp.where(past, gate, NEG_INF)
    _, idx = lax.top_k(gate, ks)
    valid = idx < qblock[:, None]

    nc = T // MOBA_Q_CHUNK
    q_c = q5.reshape(B, nc, MOBA_Q_CHUNK, N_KV_HEADS, GROUP, dh).transpose(1, 0, 3, 4, 2, 5)
    idx_c = jnp.moveaxis(idx.reshape(B, N_KV_HEADS, GROUP, nc, MOBA_Q_CHUNK, ks), 3, 0)
    val_c = jnp.moveaxis(valid.reshape(B, N_KV_HEADS, GROUP, nc, MOBA_Q_CHUNK, ks), 3, 0)
    t0s = jnp.arange(nc, dtype=jnp.int32) * MOBA_Q_CHUNK
    b_ix = jnp.arange(B)[:, None, None, None, None]
    kv_ix = jnp.arange(N_KV_HEADS)[None, :, None, None, None]
    slopes4 = slopes[:, :, None, None]
    slopes5 = slopes[:, :, None, None, None]
    blk_pos = jnp.arange(MOBA_BLOCK)

    def step(args):
        qc, ic, vc, t0 = args
        pos_t = t0 + jnp.arange(MOBA_Q_CHUNK)
        k_sel = kblk[b_ix, kv_ix, ic]
        v_sel = vblk[b_ix, kv_ix, ic]
        s_sel = jnp.einsum('bkgqd,bkgqjsd->bkgqjs', qc, k_sel).astype(jnp.float32) * scale
        dist_sel = (pos_t[:, None, None] - (ic[..., None] * MOBA_BLOCK + blk_pos)).astype(jnp.float32)
        s_sel = jnp.where(vc[..., None], s_sel - slopes5 * dist_sel, NEG_INF)
        s_sel = s_sel.reshape(B, N_KV_HEADS, GROUP, MOBA_Q_CHUNK, ks * MOBA_BLOCK)
        j0 = t0 // MOBA_BLOCK
        k_own = lax.dynamic_index_in_dim(kblk, j0, axis=2, keepdims=False)
        v_own = lax.dynamic_index_in_dim(vblk, j0, axis=2, keepdims=False)
        dist_own = pos_t[:, None] - (j0 * MOBA_BLOCK + blk_pos)[None, :]
        s_own = jnp.einsum('bkgqd,bksd->bkgqs', qc, k_own).astype(jnp.float32) * scale
        s_own = jnp.where(dist_own >= 0, s_own - slopes4 * dist_own.astype(jnp.float32), NEG_INF)
        p = jax.nn.softmax(jnp.concatenate([s_sel, s_own], axis=-1), axis=-1).astype(vblk.dtype)
        p_sel = p[..., :ks * MOBA_BLOCK].reshape(B, N_KV_HEADS, GROUP, MOBA_Q_CHUNK, ks, MOBA_BLOCK)
        p_own = p[..., ks * MOBA_BLOCK:]
        return (jnp.einsum('bkgqjs,bkgqjsd->bkgqd', p_sel, v_sel)
                + jnp.einsum('bkgqs,bksd->bkgqd', p_own, v_own))

    o = lax.map(step, (q_c, idx_c, val_c, t0s))
    return o.transpose(1, 0, 4, 2, 3, 5).reshape(B, T, H * dh)


def setup_inputs(seed: int = 0) -> dict:
    key = jax.random.key(seed)
    ks = jax.random.split(key, 16)
    f32 = jnp.float32

    def w(k, shape, fan_in):
        return jax.random.normal(k, shape, f32) * (fan_in ** -0.5)

    def gain(k, shape):
        return 1.0 + 0.05 * jax.random.normal(k, shape, f32)

    return {
        "x": jax.random.normal(ks[0], (BATCH, SEQ, D_MODEL), f32),
        "w_qkv_a": w(ks[1], (N_A_LAYERS, D_MODEL, Q_DIM + 2 * KV_DIM), D_MODEL),
        "sinks_a": 0.5 * jax.random.normal(ks[2], (N_A_LAYERS, N_HEADS), f32),
        "w_o_a": w(ks[3], (N_A_LAYERS, Q_DIM, D_MODEL), Q_DIM),
        "kv_norm": gain(ks[4], (D_MODEL,)),
        "w_kv_shared": w(ks[5], (D_MODEL, 2 * KV_DIM), D_MODEL),
        "w_q_b": w(ks[6], (N_B_LAYERS, D_MODEL, Q_DIM), D_MODEL),
        "w_o_b": w(ks[7], (N_B_LAYERS, Q_DIM, D_MODEL), Q_DIM),
        "norm_attn_pre": gain(ks[8], (DEPTH, D_MODEL)),
        "norm_attn_post": gain(ks[9], (DEPTH, D_MODEL)),
        "norm_mlp_pre": gain(ks[10], (DEPTH, D_MODEL)),
        "norm_mlp_post": gain(ks[11], (DEPTH, D_MODEL)),
        "w_up": w(ks[12], (DEPTH, D_MODEL, D_FF), D_MODEL),
        "w_down": w(ks[13], (DEPTH, D_FF, D_MODEL), D_FF),
    }


def reference(x, w_qkv_a, sinks_a, w_o_a, kv_norm, w_kv_shared, w_q_b, w_o_b,
              norm_attn_pre, norm_attn_post, norm_mlp_pre, norm_mlp_post, w_up, w_down):
    B, T, _ = x.shape
    slopes = alibi_slopes(N_HEADS).reshape(N_KV_HEADS, GROUP)
    shared = None
    for l in range(DEPTH):
        h = rmsnorm(x, norm_attn_pre[l])
        if l < N_A_LAYERS:
            qkv = h @ w_qkv_a[l]
            q, k, v = jnp.split(qkv, [Q_DIM, Q_DIM + KV_DIM], axis=-1)
            mix = sliding_window_sink_attention(
                q.reshape(B, T, N_HEADS, HEAD_DIM),
                k.reshape(B, T, N_KV_HEADS, HEAD_DIM),
                v.reshape(B, T, N_KV_HEADS, HEAD_DIM),
                sinks_a[l], slopes) @ w_o_a[l]
        else:
            j = l - N_A_LAYERS
            q = (h @ w_q_b[j]).reshape(B, T, N_HEADS, HEAD_DIM)
            kblk, vblk, kmean = shared
            mix = moba_attention(q, kblk, vblk, kmean, slopes) @ w_o_b[j]
        x = x + rmsnorm(mix, norm_attn_post[l])
        h = rmsnorm(x, norm_mlp_pre[l])
        x = x + rmsnorm(sq_relu_mlp(h, w_up[l], w_down[l]), norm_mlp_post[l])
        if l == N_A_LAYERS - 1:
            shared = moba_shared_kv_side(x, kv_norm, w_kv_shared)
    return x
```

```python
import functools

import jax
import jax.numpy as jnp
from jax import lax
from jax.experimental import pallas as pl
from jax.experimental.pallas import tpu as pltpu

D_MODEL = 2048
N_HEADS = 32
N_KV_HEADS = 4
HEAD_DIM = 64
GROUP = N_HEADS // N_KV_HEADS
PAIRS = GROUP // 2
PAIR_W = 2 * HEAD_DIM
Q_DIM = N_HEADS * HEAD_DIM
KV_DIM = N_KV_HEADS * HEAD_DIM
KV_DUP = N_KV_HEADS * PAIR_W
D_FF = 4 * D_MODEL
WINDOW = 128
MOBA_BLOCK = 256
MOBA_TOPK = 3
RMS_EPS = 1e-6
NEG_INF = -1e30
BELOW_NEG_INF = -3e38

SEL_LANES = 16
SLOPE_BLK_LANE = 16
SLOPE_POS_LANE = 19
SLOPE_PIECES = 3

VMEM_LIMIT = 56 * 1024 * 1024

BF16 = jnp.bfloat16
F32 = jnp.float32
NT_DIMS = (((1,), (1,)), ((), ()))


def _rms_scale(x):
    return lax.rsqrt(jnp.mean(x * x, axis=-1, keepdims=True) + RMS_EPS)


def _cparams(semantics):
    return pltpu.CompilerParams(dimension_semantics=semantics,
                                vmem_limit_bytes=VMEM_LIMIT)


PROJ_COL_CHUNK = 512


def _norm_proj_kernel(x_ref, g_ref, w_ref, o_ref):
    x = x_ref[...]
    h = ((x * _rms_scale(x)) * g_ref[...]).astype(BF16)
    for c in range(0, w_ref.shape[1], PROJ_COL_CHUNK):
        o_ref[:, c:c + PROJ_COL_CHUNK] = jnp.dot(
            h, w_ref[:, c:c + PROJ_COL_CHUNK],
            preferred_element_type=F32).astype(o_ref.dtype)


def norm_proj(x, g, w, *, tm, name):
    n, d = x.shape
    n_out = w.shape[1]
    return pl.pallas_call(
        _norm_proj_kernel,
        grid=(n // tm,),
        in_specs=[pl.BlockSpec((tm, d), lambda i: (i, 0)),
                  pl.BlockSpec((1, d), lambda i: (0, 0)),
                  pl.BlockSpec((d, n_out), lambda i: (0, 0))],
        out_specs=pl.BlockSpec((tm, n_out), lambda i: (i, 0)),
        out_shape=jax.ShapeDtypeStruct((n, n_out), BF16),
        compiler_params=_cparams(("parallel",)),
        name=name,
    )(x, g.reshape(1, d), w)


def _kv_proj_kernel(x_ref, g_ref, w_ref, k_ref, v_ref, km_ref):
    x = x_ref[...]
    h = ((x * _rms_scale(x)) * g_ref[...]).astype(BF16)
    kacc = jnp.dot(h, w_ref[:, :KV_DUP], preferred_element_type=F32)
    k_ref[...] = kacc.astype(BF16)
    v_ref[...] = jnp.dot(h, w_ref[:, KV_DUP:], preferred_element_type=F32).astype(BF16)
    for blk in range(x_ref.shape[0] // MOBA_BLOCK):
        rows = kacc[blk * MOBA_BLOCK:(blk + 1) * MOBA_BLOCK]
        km_ref[blk] = jnp.sum(rows, axis=0, keepdims=True) * (1.0 / MOBA_BLOCK)


def kv_proj(x, g, w, *, tm):
    n, d = x.shape
    nb = tm // MOBA_BLOCK
    return pl.pallas_call(
        _kv_proj_kernel,
        grid=(n // tm,),
        in_specs=[pl.BlockSpec((tm, d), lambda i: (i, 0)),
                  pl.BlockSpec((1, d), lambda i: (0, 0)),
                  pl.BlockSpec((d, 2 * KV_DUP), lambda i: (0, 0))],
        out_specs=[pl.BlockSpec((tm, KV_DUP), lambda i: (i, 0)),
                   pl.BlockSpec((tm, KV_DUP), lambda i: (i, 0)),
                   pl.BlockSpec((nb, 1, KV_DUP), lambda i: (i, 0, 0))],
        out_shape=[jax.ShapeDtypeStruct((n, KV_DUP), BF16),
                   jax.ShapeDtypeStruct((n, KV_DUP), BF16),
                   jax.ShapeDtypeStruct((n // MOBA_BLOCK, 1, KV_DUP), F32)],
        compiler_params=_cparams(("parallel",)),
        name="kv_proj",
    )(x, g.reshape(1, d), w)


def _oproj_kernel(mix_ref, x_ref, g_ref, w_ref, o_ref):
    a = jnp.dot(mix_ref[...], w_ref[...], preferred_element_type=F32)
    o_ref[...] = x_ref[...] + (a * _rms_scale(a)) * g_ref[...]


def oproj_norm_residual(mix, x, g, w, *, tm, name):
    n, d = x.shape
    return pl.pallas_call(
        _oproj_kernel,
        grid=(n // tm,),
        in_specs=[pl.BlockSpec((tm, mix.shape[1]), lambda i: (i, 0)),
                  pl.BlockSpec((tm, d), lambda i: (i, 0)),
                  pl.BlockSpec((1, d), lambda i: (0, 0)),
                  pl.BlockSpec(w.shape, lambda i: (0, 0))],
        out_specs=pl.BlockSpec((tm, d), lambda i: (i, 0)),
        out_shape=jax.ShapeDtypeStruct((n, d), F32),
        compiler_params=_cparams(("parallel",)),
        name=name,
    )(mix, x, g.reshape(1, d), w)


def _mlp_kernel(x_ref, gpre_ref, gpost_ref, wup_ref, wdn_ref, o_ref, h_ref, acc_ref):
    f = pl.program_id(1)

    @pl.when(f == 0)
    def _():
        x = x_ref[...]
        h_ref[...] = ((x * _rms_scale(x)) * gpre_ref[...]).astype(BF16)

    u = jnp.dot(h_ref[...], wup_ref[...], preferred_element_type=F32)
    u = jnp.maximum(u, 0.0)
    d = jnp.dot((u * u).astype(BF16), wdn_ref[...], preferred_element_type=F32)

    @pl.when(f == 0)
    def _():
        acc_ref[...] = d

    @pl.when(f > 0)
    def _():
        acc_ref[...] += d

    @pl.when(f == pl.num_programs(1) - 1)
    def _():
        a = acc_ref[...]
        o_ref[...] = x_ref[...] + (a * _rms_scale(a)) * gpost_ref[...]


def mlp_block(x, gpre, gpost, wup, wdn, *, tm, tf, name):
    n, d = x.shape
    dff = wup.shape[1]
    return pl.pallas_call(
        _mlp_kernel,
        grid=(n // tm, dff // tf),
        in_specs=[pl.BlockSpec((tm, d), lambda i, f: (i, 0)),
                  pl.BlockSpec((1, d), lambda i, f: (0, 0)),
                  pl.BlockSpec((1, d), lambda i, f: (0, 0)),
                  pl.BlockSpec((d, tf), lambda i, f: (0, f)),
                  pl.BlockSpec((tf, d), lambda i, f: (f, 0))],
        out_specs=pl.BlockSpec((tm, d), lambda i, f: (i, 0)),
        out_shape=jax.ShapeDtypeStruct((n, d), F32),
        scratch_shapes=[pltpu.VMEM((tm, d), BF16), pltpu.VMEM((tm, d), F32)],
        compiler_params=_cparams(("parallel", "arbitrary")),
        name=name,
    )(x, gpre.reshape(1, d), gpost.reshape(1, d), wup, wdn)


def _stack_pairs(q):
    return jnp.concatenate([q[:, p * PAIR_W:(p + 1) * PAIR_W] for p in range(PAIRS)], axis=0)


def _half(x, parity):
    lane = lax.broadcasted_iota(jnp.int32, x.shape, 1)
    keep = (lane < HEAD_DIM) if parity == 0 else (lane >= HEAD_DIM)
    return jnp.where(keep, x, jnp.zeros_like(x))


def _by_parity(even, odd, shape):
    lane = lax.broadcasted_iota(jnp.int32, shape, 1)
    return jnp.where(lane < HEAD_DIM, even, odd)


def _swa_kernel(slopes_ref, sinks_ref, q_ref, kp_ref, kc_ref, vp_ref, vc_ref, o_ref):
    kh = pl.program_id(1)
    i = pl.program_id(2)
    qs = _stack_pairs(q_ref[...])
    kk = jnp.concatenate([kp_ref[...], kc_ref[...]], axis=0)
    vv = jnp.concatenate([vp_ref[...], vc_ref[...]], axis=0)
    ti = lax.broadcasted_iota(jnp.int32, (WINDOW, 2 * WINDOW), 0)
    sj = lax.broadcasted_iota(jnp.int32, (WINDOW, 2 * WINDOW), 1)
    dist = ti + WINDOW - sj
    lowest_key = jnp.where(i > 0, 0, WINDOW)
    in_seq = jnp.where(sj >= lowest_key, dist, -1)
    keep = (in_seq >= 0) & (in_seq < WINDOW)
    distf = dist.astype(F32)

    outs = []
    for parity in range(2):
        s = lax.dot_general(qs, _half(kk, parity), NT_DIMS, preferred_element_type=F32)
        probs, inv_l = [], []
        for p in range(PAIRS):
            head = kh * GROUP + 2 * p + parity
            slope = slopes_ref[head]
            sink = sinks_ref[head]
            sp = s[p * WINDOW:(p + 1) * WINDOW]
            sp = jnp.where(keep, sp - slope * distf, NEG_INF)
            m = jnp.maximum(jnp.max(sp, axis=-1, keepdims=True), sink)
            e = jnp.exp(sp - m)
            l = jnp.sum(e, axis=-1, keepdims=True) + jnp.exp(sink - m)
            probs.append(e.astype(BF16))
            inv_l.append(jnp.broadcast_to(1.0 / l, (WINDOW, PAIR_W)))
        o = jnp.dot(jnp.concatenate(probs, axis=0), _half(vv, parity),
                    preferred_element_type=F32)
        outs.append(o * jnp.concatenate(inv_l, axis=0))
    out = outs[0] + outs[1]
    for p in range(PAIRS):
        o_ref[:, p * PAIR_W:(p + 1) * PAIR_W] = out[p * WINDOW:(p + 1) * WINDOW].astype(o_ref.dtype)


def swa_attention(q, kdup, vdup, slopes, sinks, *, batch, seq):
    n = q.shape[0]
    nb = seq // WINDOW
    qspec = pl.BlockSpec((WINDOW, PAIRS * PAIR_W), lambda b, kh, i: (b * nb + i, kh))
    cur = pl.BlockSpec((WINDOW, PAIR_W), lambda b, kh, i: (b * nb + i, kh))
    prev = pl.BlockSpec((WINDOW, PAIR_W), lambda b, kh, i: (jnp.maximum(b * nb + i - 1, 0), kh))
    smem = pl.BlockSpec(memory_space=pltpu.SMEM)
    return pl.pallas_call(
        _swa_kernel,
        grid=(batch, N_KV_HEADS, nb),
        in_specs=[smem, smem, qspec, prev, cur, prev, cur],
        out_specs=qspec,
        out_shape=jax.ShapeDtypeStruct((n, Q_DIM), BF16),
        compiler_params=_cparams(("parallel", "parallel", "arbitrary")),
        name="swa_attention",
    )(slopes, sinks, q, kdup, kdup, vdup, vdup)


def _moba_key_features(n_mask, blk_off):
    shape = (MOBA_BLOCK, PAIR_W)
    lane = lax.broadcasted_iota(jnp.int32, shape, 1)
    pos = lax.broadcasted_iota(jnp.int32, shape, 0).astype(F32)
    blk = (blk_off * MOBA_BLOCK).astype(F32)
    feat = jnp.where(lane == n_mask, NEG_INF, 0.0)
    feat = jnp.where(lane >= SLOPE_BLK_LANE, jnp.where(lane < SLOPE_POS_LANE, blk, feat), feat)
    feat = jnp.where(lane >= SLOPE_POS_LANE,
                     jnp.where(lane < SLOPE_POS_LANE + SLOPE_PIECES, pos, feat), feat)
    return feat.astype(BF16)


def _moba_kernel(tab_ref, q_ref, k_ref, v_ref, km_ref, o_ref,
                 lhs_ref, acc_ref, m_ref, l_ref):
    j = pl.program_id(2)
    rows = PAIRS * MOBA_BLOCK
    qs = _stack_pairs(q_ref[...])

    km = jnp.concatenate(
        [km_ref[0], jnp.zeros((PAIR_W - km_ref.shape[1], PAIR_W), BF16)], axis=0)
    lane = lax.broadcasted_iota(jnp.int32, (rows, PAIR_W), 1)
    lanef = lane.astype(F32)
    for parity in range(2):
        gate = lax.dot_general(qs, _half(km, parity), NT_DIMS, preferred_element_type=F32)
        g = jnp.where(lane < j, gate, NEG_INF)
        sel = jnp.zeros((rows, PAIR_W), F32)
        for _ in range(MOBA_TOPK):
            mx = jnp.max(g, axis=-1, keepdims=True)
            idx = jnp.min(jnp.where(g == mx, lanef, float(PAIR_W)), axis=-1, keepdims=True)
            hit = lanef == idx
            sel = jnp.where(hit, 1.0, sel)
            g = jnp.where(hit, BELOW_NEG_INF, g)
        not_sel = jnp.where(lane < j, 1.0 - sel, 1.0)
        slope_feat = jnp.concatenate(
            [jnp.broadcast_to(tab_ref[0, parity * PAIRS + p:parity * PAIRS + p + 1, :],
                              (MOBA_BLOCK, PAIR_W)) for p in range(PAIRS)], axis=0)
        feat = jnp.where(lane < SEL_LANES, not_sel, slope_feat)
        lhs_ref[parity] = jnp.concatenate([qs, feat.astype(BF16)], axis=1)

    def scores(parity, kblk, kfeat):
        rhs = jnp.concatenate([_half(kblk, parity), kfeat], axis=1)
        return lax.dot_general(lhs_ref[parity], rhs, NT_DIMS, preferred_element_type=F32)

    k_own = k_ref[pl.ds(pl.multiple_of(j * MOBA_BLOCK, MOBA_BLOCK), MOBA_BLOCK), :]
    v_own = v_ref[pl.ds(pl.multiple_of(j * MOBA_BLOCK, MOBA_BLOCK), MOBA_BLOCK), :]
    kfeat_own = _moba_key_features(-1, j - j)
    tq = lax.broadcasted_iota(jnp.int32, (rows, MOBA_BLOCK), 0) & (MOBA_BLOCK - 1)
    sk = lax.broadcasted_iota(jnp.int32, (rows, MOBA_BLOCK), 1)
    causal = sk <= tq
    pv = []
    for parity in range(2):
        s = jnp.where(causal, scores(parity, k_own, kfeat_own), NEG_INF)
        m = jnp.max(s, axis=-1, keepdims=True)
        e = jnp.exp(s - m)
        m_ref[parity] = m
        l_ref[parity] = jnp.sum(e, axis=-1, keepdims=True)
        pv.append(jnp.dot(e.astype(BF16), _half(v_own, parity), preferred_element_type=F32))
    acc_ref[...] = pv[0] + pv[1]

    def past_block(n, carry):
        start = pl.multiple_of(n * MOBA_BLOCK, MOBA_BLOCK)
        kblk = k_ref[pl.ds(start, MOBA_BLOCK), :]
        vblk = v_ref[pl.ds(start, MOBA_BLOCK), :]
        kfeat = _moba_key_features(n, n - j)
        alphas, pv = [], []
        for parity in range(2):
            s = scores(parity, kblk, kfeat)
            m_old = m_ref[parity]
            m_new = jnp.maximum(m_old, jnp.max(s, axis=-1, keepdims=True))
            alpha = jnp.exp(m_old - m_new)
            e = jnp.exp(s - m_new)
            m_ref[parity] = m_new
            l_ref[parity] = alpha * l_ref[parity] + jnp.sum(e, axis=-1, keepdims=True)
            alphas.append(alpha)
            pv.append(jnp.dot(e.astype(BF16), _half(vblk, parity), preferred_element_type=F32))
        alpha = _by_parity(alphas[0], alphas[1], (rows, PAIR_W))
        acc_ref[...] = acc_ref[...] * alpha + (pv[0] + pv[1])
        return carry

    lax.fori_loop(0, j, past_block, 0)

    out = acc_ref[...] / _by_parity(l_ref[0], l_ref[1], (rows, PAIR_W))
    for p in range(PAIRS):
        o_ref[:, p * PAIR_W:(p + 1) * PAIR_W] = (
            out[p * MOBA_BLOCK:(p + 1) * MOBA_BLOCK].astype(o_ref.dtype))


def moba_attention(q, kdup, vdup, kmean, slope_tab, *, batch, seq):
    n = q.shape[0]
    nblk = seq // MOBA_BLOCK
    rows = PAIRS * MOBA_BLOCK
    qspec = pl.BlockSpec((MOBA_BLOCK, PAIRS * PAIR_W), lambda b, kh, j: (b * nblk + j, kh))
    kvspec = pl.BlockSpec((seq, PAIR_W), lambda b, kh, j: (b, kh))
    return pl.pallas_call(
        _moba_kernel,
        grid=(batch, N_KV_HEADS, nblk),
        in_specs=[pl.BlockSpec((1, 2 * PAIRS, PAIR_W), lambda b, kh, j: (kh, 0, 0)),
                  qspec, kvspec, kvspec,
                  pl.BlockSpec((1, nblk, PAIR_W), lambda b, kh, j: (b, 0, kh))],
        out_specs=qspec,
        out_shape=jax.ShapeDtypeStruct((n, Q_DIM), BF16),
        scratch_shapes=[pltpu.VMEM((2, rows, 2 * PAIR_W), BF16),
                        pltpu.VMEM((rows, PAIR_W), F32),
                        pltpu.VMEM((2, rows, 1), F32),
                        pltpu.VMEM((2, rows, 1), F32)],
        compiler_params=_cparams(("parallel", "parallel", "arbitrary")),
        name="moba_attention",
    )(slope_tab, q, kdup, vdup, kmean)


def _dup_heads(w):
    d = w.shape[0]
    w4 = w.reshape(d, N_KV_HEADS, HEAD_DIM)
    return jnp.concatenate([w4, w4], axis=-1).reshape(d, KV_DUP)


def _alibi_slopes():
    return jnp.exp2(-8.0 * jnp.arange(1, N_HEADS + 1, dtype=F32) / N_HEADS)


def _slope_feature_table(slopes):
    hi = slopes.astype(BF16).astype(F32)
    mid = (slopes - hi).astype(BF16).astype(F32)
    lo = (slopes - hi - mid).astype(BF16).astype(F32)
    pieces = jnp.stack([hi, mid, lo], axis=-1)
    row = jnp.zeros((N_HEADS, PAIR_W), F32)
    row = row.at[:, SLOPE_BLK_LANE:SLOPE_BLK_LANE + SLOPE_PIECES].set(pieces)
    row = row.at[:, SLOPE_POS_LANE:SLOPE_POS_LANE + SLOPE_PIECES].set(pieces)
    row = row.reshape(N_KV_HEADS, PAIRS, 2, PAIR_W).transpose(0, 2, 1, 3)
    return row.reshape(N_KV_HEADS, 2 * PAIRS, PAIR_W)


def kernel(x, w_qkv_a, sinks_a, w_o_a, kv_norm, w_kv_shared, w_q_b, w_o_b,
           norm_attn_pre, norm_attn_post, norm_mlp_pre, norm_mlp_post, w_up, w_down):
    batch, seq, d = x.shape
    n = batch * seq
    scale = HEAD_DIM ** -0.5
    slopes = _alibi_slopes()

    wq_a, wk_a, wv_a = jnp.split(w_qkv_a[0], [Q_DIM, Q_DIM + KV_DIM], axis=-1)
    w_qkv = jnp.concatenate([wq_a * scale, _dup_heads(wk_a), _dup_heads(wv_a)], axis=-1).astype(BF16)
    wk_s, wv_s = jnp.split(w_kv_shared, [KV_DIM], axis=-1)
    w_kv = jnp.concatenate([_dup_heads(wk_s), _dup_heads(wv_s)], axis=-1).astype(BF16)
    w_qb = (w_q_b[0] * scale).astype(BF16)

    xf = x.reshape(n, d)

    qkv = norm_proj(xf, norm_attn_pre[0], w_qkv, tm=512, name="qkv_proj")
    q0, k0, v0 = qkv[:, :Q_DIM], qkv[:, Q_DIM:Q_DIM + KV_DUP], qkv[:, Q_DIM + KV_DUP:]
    mix0 = swa_attention(q0, k0, v0, slopes, sinks_a[0].astype(F32), batch=batch, seq=seq)
    x1 = oproj_norm_residual(mix0, xf, norm_attn_post[0], w_o_a[0].astype(BF16), tm=512, name="oproj0")
    x2 = mlp_block(x1, norm_mlp_pre[0], norm_mlp_post[0],
                   w_up[0].astype(BF16), w_down[0].astype(BF16), tm=512, tf=1024, name="mlp0")

    kdup, vdup, kmean = kv_proj(x2, kv_norm, w_kv, tm=512)
    kmean = kmean.reshape(batch, seq // MOBA_BLOCK, KV_DUP).astype(BF16)

    q1 = norm_proj(x2, norm_attn_pre[1], w_qb, tm=512, name="q_proj1")
    mix1 = moba_attention(q1, kdup, vdup, kmean, _slope_feature_table(slopes), batch=batch, seq=seq)
    x3 = oproj_norm_residual(mix1, x2, norm_attn_post[1], w_o_b[0].astype(BF16), tm=512, name="oproj1")
    x4 = mlp_block(x3, norm_mlp_pre[1], norm_mlp_post[1],
                   w_up[1].astype(BF16), w_down[1].astype(BF16), tm=512, tf=1024, name="mlp1")
    return x4.reshape(batch, seq, d)
```

```python
import jax
import jax.numpy as jnp
from jax import lax
from jax.experimental import pallas as pl
from jax.experimental.pallas import tpu as pltpu

D_MODEL = 2048
N_HEADS = 32
N_KV_HEADS = 4
HEAD_DIM = 64
GROUP = N_HEADS // N_KV_HEADS
PAIRS = GROUP // 2
PAIR_W = 2 * HEAD_DIM
Q_DIM = N_HEADS * HEAD_DIM
KV_DIM = N_KV_HEADS * HEAD_DIM
KV_DUP = N_KV_HEADS * PAIR_W
D_FF = 4 * D_MODEL
WINDOW = 128
MOBA_BLOCK = 256
MOBA_TOPK = 3
RMS_EPS = 1e-6
NEG_INF = -1e30
BELOW_NEG_INF = -3e38

SEL_LANES = 16
SLOPE_BLK_LANE = 16
SLOPE_POS_LANE = 19
SLOPE_PIECES = 3

VMEM_LIMIT = 56 * 1024 * 1024

BF16 = jnp.bfloat16
F32 = jnp.float32
NT_DIMS = (((1,), (1,)), ((), ()))


def _rms_scale(x):
    return lax.rsqrt(jnp.mean(x * x, axis=-1, keepdims=True) + RMS_EPS)


def _cparams(semantics):
    return pltpu.CompilerParams(dimension_semantics=semantics,
                                vmem_limit_bytes=VMEM_LIMIT)


PROJ_CHUNK = 512


def _normed(x_ref, g_ref):
    x = x_ref[...]
    return ((x * _rms_scale(x)) * g_ref[...]).astype(BF16)


def _qkv_proj_kernel(x_ref, g_ref, wq_ref, wk_ref, wv_ref, q_ref, k_ref, v_ref):
    h = _normed(x_ref, g_ref)
    for c in range(0, wq_ref.shape[1], PROJ_CHUNK):
        q_ref[:, c:c + PROJ_CHUNK] = jnp.dot(
            h, wq_ref[:, c:c + PROJ_CHUNK], preferred_element_type=F32).astype(BF16)
    k_ref[...] = jnp.dot(h, wk_ref[...], preferred_element_type=F32).astype(BF16)
    v_ref[...] = jnp.dot(h, wv_ref[...], preferred_element_type=F32).astype(BF16)


def qkv_proj(x, g, wq, wk, wv, *, tm):
    n, d = x.shape
    row = lambda width: pl.BlockSpec((tm, width), lambda i: (i, 0))
    whole = lambda w: pl.BlockSpec(w.shape, lambda i: (0, 0))
    return pl.pallas_call(
        _qkv_proj_kernel,
        grid=(n // tm,),
        in_specs=[row(d), pl.BlockSpec((1, d), lambda i: (0, 0)), whole(wq), whole(wk), whole(wv)],
        out_specs=[row(wq.shape[1]), row(wk.shape[1]), row(wv.shape[1])],
        out_shape=[jax.ShapeDtypeStruct((n, w.shape[1]), BF16) for w in (wq, wk, wv)],
        compiler_params=_cparams(("parallel",)),
        name="qkv_proj",
    )(x, g.reshape(1, d), wq, wk, wv)


def _q_proj_t_kernel(x_ref, g_ref, wt_ref, o_ref):
    h = _normed(x_ref, g_ref)
    for c in range(0, wt_ref.shape[0], PROJ_CHUNK):
        o_ref[c:c + PROJ_CHUNK, :] = lax.dot_general(
            wt_ref[c:c + PROJ_CHUNK, :], h, NT_DIMS, preferred_element_type=F32).astype(BF16)


def q_proj_t(x, g, wt, *, tm):
    n, d = x.shape
    n_out = wt.shape[0]
    return pl.pallas_call(
        _q_proj_t_kernel,
        grid=(n // tm,),
        in_specs=[pl.BlockSpec((tm, d), lambda i: (i, 0)),
                  pl.BlockSpec((1, d), lambda i: (0, 0)),
                  pl.BlockSpec((n_out, d), lambda i: (0, 0))],
        out_specs=pl.BlockSpec((n_out, tm), lambda i: (0, i)),
        out_shape=jax.ShapeDtypeStruct((n_out, n), BF16),
        compiler_params=_cparams(("parallel",)),
        name="q_proj_t",
    )(x, g.reshape(1, d), wt)


def _kv_proj_kernel(x_ref, g_ref, wk_ref, wvt_ref, feat_ref, k_ref, vt_ref, km_ref):
    h = _normed(x_ref, g_ref)
    nblk = x_ref.shape[0] // MOBA_BLOCK
    kacc = jnp.dot(h, wk_ref[...], preferred_element_type=F32)
    for blk in range(nblk):
        rows = kacc[blk * MOBA_BLOCK:(blk + 1) * MOBA_BLOCK]
        km_ref[blk] = jnp.sum(rows, axis=0, keepdims=True) * (1.0 / MOBA_BLOCK)
    feat = feat_ref[...]
    for kh in range(N_KV_HEADS):
        cols = slice(kh * PAIR_W, (kh + 1) * PAIR_W)
        k_ref[:, cols] = (kacc[:, cols] + feat).astype(BF16)
    vt = lax.dot_general(wvt_ref[...], h, NT_DIMS, preferred_element_type=F32)
    for blk in range(nblk):
        vt_ref[blk] = vt[:, blk * MOBA_BLOCK:(blk + 1) * MOBA_BLOCK].astype(BF16)


def kv_proj(x, g, wk_aug, wvt, key_feat, *, tm, seq):
    n, d = x.shape
    nb = tm // MOBA_BLOCK
    tiles_per_seq = seq // tm
    return pl.pallas_call(
        _kv_proj_kernel,
        grid=(n // tm,),
        in_specs=[pl.BlockSpec((tm, d), lambda i: (i, 0)),
                  pl.BlockSpec((1, d), lambda i: (0, 0)),
                  pl.BlockSpec(wk_aug.shape, lambda i: (0, 0)),
                  pl.BlockSpec(wvt.shape, lambda i: (0, 0)),
                  pl.BlockSpec((tm, PAIR_W), lambda i: (i % tiles_per_seq, 0))],
        out_specs=[pl.BlockSpec((tm, KV_DUP), lambda i: (i, 0)),
                   pl.BlockSpec((nb, KV_DIM, MOBA_BLOCK), lambda i: (i, 0, 0)),
                   pl.BlockSpec((nb, 1, KV_DUP), lambda i: (i, 0, 0))],
        out_shape=[jax.ShapeDtypeStruct((n, KV_DUP), BF16),
                   jax.ShapeDtypeStruct((n // MOBA_BLOCK, KV_DIM, MOBA_BLOCK), BF16),
                   jax.ShapeDtypeStruct((n // MOBA_BLOCK, 1, KV_DUP), F32)],
        compiler_params=_cparams(("parallel",)),
        name="kv_proj",
    )(x, g.reshape(1, d), wk_aug, wvt, key_feat)


def _oproj_kernel(mix_ref, x_ref, g_ref, w_ref, o_ref):
    a = jnp.dot(mix_ref[...], w_ref[...], preferred_element_type=F32)
    o_ref[...] = x_ref[...] + (a * _rms_scale(a)) * g_ref[...]


def oproj_norm_residual(mix, x, g, w, *, tm, name):
    n, d = x.shape
    return pl.pallas_call(
        _oproj_kernel,
        grid=(n // tm,),
        in_specs=[pl.BlockSpec((tm, mix.shape[1]), lambda i: (i, 0)),
                  pl.BlockSpec((tm, d), lambda i: (i, 0)),
                  pl.BlockSpec((1, d), lambda i: (0, 0)),
                  pl.BlockSpec(w.shape, lambda i: (0, 0))],
        out_specs=pl.BlockSpec((tm, d), lambda i: (i, 0)),
        out_shape=jax.ShapeDtypeStruct((n, d), F32),
        compiler_params=_cparams(("parallel",)),
        name=name,
    )(mix, x, g.reshape(1, d), w)


def _mlp_kernel(x_ref, gpre_ref, gpost_ref, wup_ref, wdn_ref, o_ref, h_ref, acc_ref):
    f = pl.program_id(1)

    @pl.when(f == 0)
    def _():
        x = x_ref[...]
        h_ref[...] = ((x * _rms_scale(x)) * gpre_ref[...]).astype(BF16)

    u = jnp.dot(h_ref[...], wup_ref[...], preferred_element_type=F32)
    u = jnp.maximum(u, 0.0)
    d = jnp.dot((u * u).astype(BF16), wdn_ref[...], preferred_element_type=F32)

    @pl.when(f == 0)
    def _():
        acc_ref[...] = d

    @pl.when(f > 0)
    def _():
        acc_ref[...] += d

    @pl.when(f == pl.num_programs(1) - 1)
    def _():
        a = acc_ref[...]
        o_ref[...] = x_ref[...] + (a * _rms_scale(a)) * gpost_ref[...]


def mlp_block(x, gpre, gpost, wup, wdn, *, tm, tf, name):
    n, d = x.shape
    dff = wup.shape[1]
    return pl.pallas_call(
        _mlp_kernel,
        grid=(n // tm, dff // tf),
        in_specs=[pl.BlockSpec((tm, d), lambda i, f: (i, 0)),
                  pl.BlockSpec((1, d), lambda i, f: (0, 0)),
                  pl.BlockSpec((1, d), lambda i, f: (0, 0)),
                  pl.BlockSpec((d, tf), lambda i, f: (0, f)),
                  pl.BlockSpec((tf, d), lambda i, f: (f, 0))],
        out_specs=pl.BlockSpec((tm, d), lambda i, f: (i, 0)),
        out_shape=jax.ShapeDtypeStruct((n, d), F32),
        scratch_shapes=[pltpu.VMEM((tm, d), BF16), pltpu.VMEM((tm, d), F32)],
        compiler_params=_cparams(("parallel", "arbitrary")),
        name=name,
    )(x, gpre.reshape(1, d), gpost.reshape(1, d), wup, wdn)


def _stack_pairs(q):
    return jnp.concatenate([q[:, p * PAIR_W:(p + 1) * PAIR_W] for p in range(PAIRS)], axis=0)


def _half(x, parity):
    lane = lax.broadcasted_iota(jnp.int32, x.shape, 1)
    keep = (lane < HEAD_DIM) if parity == 0 else (lane >= HEAD_DIM)
    return jnp.where(keep, x, jnp.zeros_like(x))


def _swa_kernel(slopes_ref, sinks_ref, q_ref, kp_ref, kc_ref, vp_ref, vc_ref, o_ref):
    kh = pl.program_id(1)
    i = pl.program_id(2)
    qs = _stack_pairs(q_ref[...])
    kk = jnp.concatenate([kp_ref[...], kc_ref[...]], axis=0)
    vv = jnp.concatenate([vp_ref[...], vc_ref[...]], axis=0)
    ti = lax.broadcasted_iota(jnp.int32, (WINDOW, 2 * WINDOW), 0)
    sj = lax.broadcasted_iota(jnp.int32, (WINDOW, 2 * WINDOW), 1)
    dist = ti + WINDOW - sj
    lowest_key = jnp.where(i > 0, 0, WINDOW)
    in_seq = jnp.where(sj >= lowest_key, dist, -1)
    keep = (in_seq >= 0) & (in_seq < WINDOW)
    distf = dist.astype(F32)

    outs = []
    for parity in range(2):
        s = lax.dot_general(qs, _half(kk, parity), NT_DIMS, preferred_element_type=F32)
        probs, inv_l = [], []
        for p in range(PAIRS):
            head = kh * GROUP + 2 * p + parity
            slope = slopes_ref[head]
            sink = sinks_ref[head]
            sp = s[p * WINDOW:(p + 1) * WINDOW]
            sp = jnp.where(keep, sp - slope * distf, NEG_INF)
            m = jnp.maximum(jnp.max(sp, axis=-1, keepdims=True), sink)
            e = jnp.exp(sp - m)
            l = jnp.sum(e, axis=-1, keepdims=True) + jnp.exp(sink - m)
            probs.append(e.astype(BF16))
            inv_l.append(jnp.broadcast_to(1.0 / l, (WINDOW, PAIR_W)))
        o = jnp.dot(jnp.concatenate(probs, axis=0), _half(vv, parity),
                    preferred_element_type=F32)
        outs.append(o * jnp.concatenate(inv_l, axis=0))
    out = outs[0] + outs[1]
    for p in range(PAIRS):
        o_ref[:, p * PAIR_W:(p + 1) * PAIR_W] = out[p * WINDOW:(p + 1) * WINDOW].astype(o_ref.dtype)


def swa_attention(q, kdup, vdup, slopes, sinks, *, batch, seq):
    n = q.shape[0]
    nb = seq // WINDOW
    qspec = pl.BlockSpec((WINDOW, PAIRS * PAIR_W), lambda b, kh, i: (b * nb + i, kh))
    cur = pl.BlockSpec((WINDOW, PAIR_W), lambda b, kh, i: (b * nb + i, kh))
    prev = pl.BlockSpec((WINDOW, PAIR_W), lambda b, kh, i: (jnp.maximum(b * nb + i - 1, 0), kh))
    smem = pl.BlockSpec(memory_space=pltpu.SMEM)
    return pl.pallas_call(
        _swa_kernel,
        grid=(batch, N_KV_HEADS, nb),
        in_specs=[smem, smem, qspec, prev, cur, prev, cur],
        out_specs=qspec,
        out_shape=jax.ShapeDtypeStruct((n, Q_DIM), BF16),
        compiler_params=_cparams(("parallel", "parallel", "arbitrary")),
        name="swa_attention",
    )(slopes, sinks, q, kdup, kdup, vdup, vdup)


ONES_ROWS = 16


def _moba_kernel(slope_ref, q_ref, k_ref, vt_ref, km_ref, o_ref, lhs_ref, acc_ref, m_ref):
    j = pl.program_id(2)
    lanes = GROUP * MOBA_BLOCK
    nblk = km_ref.shape[1]
    qt = q_ref[...]

    km = km_ref[0][:, :HEAD_DIM]
    gate = jnp.concatenate(
        [jnp.dot(km, qt[h * HEAD_DIM:(h + 1) * HEAD_DIM, :], preferred_element_type=F32)
         for h in range(GROUP)], axis=1)
    blk = lax.broadcasted_iota(jnp.int32, (nblk, lanes), 0)
    blkf = blk.astype(F32)
    g = jnp.where(blk < j, gate, NEG_INF)
    sel = jnp.zeros((nblk, lanes), F32)
    for _ in range(MOBA_TOPK):
        mx = jnp.max(g, axis=0, keepdims=True)
        idx = jnp.min(jnp.where(g == mx, blkf, float(nblk)), axis=0, keepdims=True)
        hit = blkf == idx
        sel = jnp.where(hit, 1.0, sel)
        g = jnp.where(hit, BELOW_NEG_INF, g)
    not_sel = jnp.where(blk < j, 1.0 - sel, jnp.where(blk == j, 0.0, 1.0))
    pad = jnp.zeros((HEAD_DIM - SEL_LANES - slope_ref.shape[1], lanes), F32)
    feat = jnp.concatenate([not_sel, slope_ref[0], pad], axis=0).astype(BF16)
    for h in range(GROUP):
        lhs_ref[0:HEAD_DIM, h * MOBA_BLOCK:(h + 1) * MOBA_BLOCK] = qt[h * HEAD_DIM:(h + 1) * HEAD_DIM, :]
    lhs_ref[HEAD_DIM:2 * HEAD_DIM, :] = feat

    row = lax.broadcasted_iota(jnp.int32, (ONES_ROWS, MOBA_BLOCK), 0)
    ones_rows = jnp.where(row == 0, 1.0, 0.0).astype(BF16)

    def attend(n, own):
        start = pl.multiple_of(n * MOBA_BLOCK, MOBA_BLOCK)
        s = jnp.dot(k_ref[pl.ds(start, MOBA_BLOCK), :], lhs_ref[...],
                    preferred_element_type=F32)
        if own:
            key = lax.broadcasted_iota(jnp.int32, s.shape, 0)
            qry = lax.broadcasted_iota(jnp.int32, s.shape, 1) & (MOBA_BLOCK - 1)
            s = jnp.where(key <= qry, s, NEG_INF)
        m_blk = jnp.max(s, axis=0, keepdims=True)
        if own:
            m_new = m_blk
        else:
            m_old = m_ref[...]
            m_new = jnp.maximum(m_old, m_blk)
            alpha = jnp.exp(m_old - m_new)
        m_ref[...] = m_new
        e = jnp.exp(s - m_new).astype(BF16)
        v_aug = jnp.concatenate([vt_ref[n], ones_rows], axis=0)
        pv = jnp.concatenate(
            [jnp.dot(v_aug, e[:, h * MOBA_BLOCK:(h + 1) * MOBA_BLOCK], preferred_element_type=F32)
             for h in range(GROUP)], axis=1)
        if own:
            acc_ref[...] = pv
        else:
            acc_ref[...] = acc_ref[...] * alpha + pv

    attend(j, True)

    def past_block(n, carry):
        attend(n, False)
        return carry

    lax.fori_loop(0, j, past_block, 0)

    out = acc_ref[0:HEAD_DIM, :] * (1.0 / acc_ref[HEAD_DIM:HEAD_DIM + 1, :])
    for p in range(PAIRS):
        pair = jnp.concatenate(
            [out[:, (2 * p) * MOBA_BLOCK:(2 * p + 1) * MOBA_BLOCK],
             out[:, (2 * p + 1) * MOBA_BLOCK:(2 * p + 2) * MOBA_BLOCK]], axis=0)
        o_ref[:, p * PAIR_W:(p + 1) * PAIR_W] = pair.T.astype(o_ref.dtype)


def moba_attention(q_t, k_aug, v_t, kmean, slope_rows, *, batch, seq):
    n = q_t.shape[1]
    nblk = seq // MOBA_BLOCK
    lanes = GROUP * MOBA_BLOCK
    return pl.pallas_call(
        _moba_kernel,
        grid=(batch, N_KV_HEADS, nblk),
        in_specs=[pl.BlockSpec((1, slope_rows.shape[1], lanes), lambda b, kh, j: (kh, 0, 0)),
                  pl.BlockSpec((GROUP * HEAD_DIM, MOBA_BLOCK), lambda b, kh, j: (kh, b * nblk + j)),
                  pl.BlockSpec((seq, PAIR_W), lambda b, kh, j: (b, kh)),
                  pl.BlockSpec((nblk, HEAD_DIM, MOBA_BLOCK), lambda b, kh, j: (b, kh, 0)),
                  pl.BlockSpec((1, nblk, PAIR_W), lambda b, kh, j: (b, 0, kh))],
        out_specs=pl.BlockSpec((MOBA_BLOCK, GROUP * HEAD_DIM), lambda b, kh, j: (b * nblk + j, kh)),
        out_shape=jax.ShapeDtypeStruct((n, Q_DIM), BF16),
        scratch_shapes=[pltpu.VMEM((2 * HEAD_DIM, lanes), BF16),
                        pltpu.VMEM((HEAD_DIM + ONES_ROWS, lanes), F32),
                        pltpu.VMEM((1, lanes), F32)],
        compiler_params=_cparams(("parallel", "parallel", "arbitrary")),
        name="moba_attention",
    )(slope_rows, q_t, k_aug, v_t, kmean)


def _dup_heads(w):
    d = w.shape[0]
    w4 = w.reshape(d, N_KV_HEADS, HEAD_DIM)
    return jnp.concatenate([w4, w4], axis=-1).reshape(d, KV_DUP)


def _pad_heads(w):
    d = w.shape[0]
    w4 = w.reshape(d, N_KV_HEADS, HEAD_DIM)
    return jnp.concatenate([w4, jnp.zeros_like(w4)], axis=-1).reshape(d, KV_DUP)


def _alibi_slopes():
    return jnp.exp2(-8.0 * jnp.arange(1, N_HEADS + 1, dtype=F32) / N_HEADS)


def _moba_slope_rows(slopes):
    hi = slopes.astype(BF16).astype(F32)
    mid = (slopes - hi).astype(BF16).astype(F32)
    lo = (slopes - hi - mid).astype(BF16).astype(F32)
    pieces = jnp.stack([hi, mid, lo], axis=-1)
    rows = jnp.concatenate([pieces, pieces, jnp.zeros((N_HEADS, 2), F32)], axis=1)
    rows = rows.reshape(N_KV_HEADS, GROUP, 8).transpose(0, 2, 1)
    return jnp.repeat(rows, MOBA_BLOCK, axis=2)


def _moba_key_features(seq):
    pos = jnp.arange(seq, dtype=jnp.int32)[:, None]
    blk = pos // MOBA_BLOCK
    lane = jnp.arange(PAIR_W, dtype=jnp.int32)[None, :] - HEAD_DIM
    feat = jnp.where(lane == blk, NEG_INF, 0.0)
    feat = jnp.where((lane >= SLOPE_BLK_LANE) & (lane < SLOPE_POS_LANE),
                     (blk * MOBA_BLOCK).astype(F32), feat)
    feat = jnp.where((lane >= SLOPE_POS_LANE) & (lane < SLOPE_POS_LANE + SLOPE_PIECES),
                     (pos % MOBA_BLOCK).astype(F32), feat)
    return jnp.where(lane < 0, 0.0, feat).astype(F32)


def kernel(x, w_qkv_a, sinks_a, w_o_a, kv_norm, w_kv_shared, w_q_b, w_o_b,
           norm_attn_pre, norm_attn_post, norm_mlp_pre, norm_mlp_post, w_up, w_down):
    batch, seq, d = x.shape
    n = batch * seq
    assert seq // MOBA_BLOCK == SEL_LANES and seq % MOBA_BLOCK == 0
    scale = HEAD_DIM ** -0.5
    slopes = _alibi_slopes()

    wq_a, wk_a, wv_a = jnp.split(w_qkv_a[0], [Q_DIM, Q_DIM + KV_DIM], axis=-1)
    wk_s, wv_s = jnp.split(w_kv_shared, [KV_DIM], axis=-1)

    xf = x.reshape(n, d)

    q0, k0, v0 = qkv_proj(xf, norm_attn_pre[0], (wq_a * scale).astype(BF16),
                          _dup_heads(wk_a).astype(BF16), _dup_heads(wv_a).astype(BF16), tm=512)
    mix0 = swa_attention(q0, k0, v0, slopes, sinks_a[0].astype(F32), batch=batch, seq=seq)
    x1 = oproj_norm_residual(mix0, xf, norm_attn_post[0], w_o_a[0].astype(BF16), tm=512, name="oproj0")
    x2 = mlp_block(x1, norm_mlp_pre[0], norm_mlp_post[0],
                   w_up[0].astype(BF16), w_down[0].astype(BF16), tm=512, tf=1024, name="mlp0")

    k_aug, v_t, kmean = kv_proj(x2, kv_norm, _pad_heads(wk_s).astype(BF16), wv_s.T.astype(BF16),
                                _moba_key_features(seq), tm=512, seq=seq)
    kmean = kmean.reshape(batch, seq // MOBA_BLOCK, KV_DUP).astype(BF16)

    q1_t = q_proj_t(x2, norm_attn_pre[1], (w_q_b[0] * scale).T.astype(BF16), tm=512)
    mix1 = moba_attention(q1_t, k_aug, v_t, kmean, _moba_slope_rows(slopes), batch=batch, seq=seq)
    x3 = oproj_norm_residual(mix1, x2, norm_attn_post[1], w_o_b[0].astype(BF16), tm=512, name="oproj1")
    x4 = mlp_block(x3, norm_mlp_pre[1], norm_mlp_post[1],
                   w_up[1].astype(BF16), w_down[1].astype(BF16), tm=512, tf=1024, name="mlp1")
    return x4.reshape(batch, seq, d)
```

```python
import functools

import jax
import jax.numpy as jnp
import numpy as np
from jax import lax
from jax.experimental import pallas as pl
from jax.experimental.pallas import tpu as pltpu

D_MODEL = 2048
N_HEADS = 32
N_KV_HEADS = 4
HEAD_DIM = 64
GROUP = N_HEADS // N_KV_HEADS
PAIRS = GROUP // 2
PAIR_W = 2 * HEAD_DIM
Q_DIM = N_HEADS * HEAD_DIM
KV_DIM = N_KV_HEADS * HEAD_DIM
K_AUG = N_KV_HEADS * PAIR_W
D_FF = 4 * D_MODEL
WINDOW = 128
MOBA_BLOCK = 256
MOBA_TOPK = 3
RMS_EPS = 1e-6
NEG_INF = -1e30
BELOW_NEG_INF = -3e38
RUNNING_MAX_INIT = -1e38
LOG2E = 1.4426950408889634

SEL_LANES = 16
SLOPE_BLK_LANE = 16
SLOPE_POS_LANE = 19
SLOPE_PIECES = 3
SLOPE_ROWS = 8
ONES_ROWS = 16

VMEM_LIMIT = 56 * 1024 * 1024

BF16 = jnp.bfloat16
F32 = jnp.float32
NT_DIMS = (((1,), (1,)), ((), ()))


def _rms_scale(x):
    return lax.rsqrt(jnp.mean(x * x, axis=-1, keepdims=True) + RMS_EPS)


def _cparams(semantics):
    return pltpu.CompilerParams(dimension_semantics=semantics,
                                vmem_limit_bytes=VMEM_LIMIT)


def _ones_rows():
    row = lax.broadcasted_iota(jnp.int32, (ONES_ROWS, MOBA_BLOCK), 0)
    return jnp.where(row == 0, 1.0, 0.0).astype(BF16)


PROJ_CHUNK = 512


def _attn_proj_kernel(x_ref, gq_ref, gkv_ref, wqt_ref, wk_ref, wvt_ref, feat_ref,
                      qt_ref, k_ref, vt_ref, *km_refs, shared_norm):
    x = x_ref[...]
    xn = x * _rms_scale(x)
    hq = (xn * gq_ref[...]).astype(BF16)
    hkv = hq if shared_norm else (xn * gkv_ref[...]).astype(BF16)
    for c in range(0, wqt_ref.shape[0], PROJ_CHUNK):
        qt_ref[c:c + PROJ_CHUNK, :] = lax.dot_general(
            wqt_ref[c:c + PROJ_CHUNK, :], hq, NT_DIMS, preferred_element_type=F32).astype(BF16)
    nblk = x_ref.shape[0] // MOBA_BLOCK
    kacc = jnp.dot(hkv, wk_ref[...], preferred_element_type=F32)
    for km_ref in km_refs:
        for blk in range(nblk):
            rows = kacc[blk * MOBA_BLOCK:(blk + 1) * MOBA_BLOCK]
            km_ref[blk] = jnp.sum(rows, axis=0, keepdims=True) * (1.0 / MOBA_BLOCK)
    feat = feat_ref[...]
    for kh in range(N_KV_HEADS):
        cols = slice(kh * PAIR_W, (kh + 1) * PAIR_W)
        k_ref[:, cols] = (kacc[:, cols] + feat).astype(BF16)
    vt = lax.dot_general(wvt_ref[...], hkv, NT_DIMS, preferred_element_type=F32)
    for blk in range(nblk):
        vt_ref[blk] = vt[:, blk * MOBA_BLOCK:(blk + 1) * MOBA_BLOCK].astype(BF16)


def attn_proj(x, gq, gkv, wq_t, wk_aug, wv_t, key_feat, *, tm, seq, shared_norm, with_kmean, name):
    n, d = x.shape
    nb = tm // MOBA_BLOCK
    tiles_per_seq = seq // tm
    whole = lambda w: pl.BlockSpec(w.shape, lambda i: (0, 0))
    gain = pl.BlockSpec((1, d), lambda i: (0, 0))
    out_specs = [pl.BlockSpec((wq_t.shape[0], tm), lambda i: (0, i)),
                 pl.BlockSpec((tm, K_AUG), lambda i: (i, 0)),
                 pl.BlockSpec((nb, KV_DIM, MOBA_BLOCK), lambda i: (i, 0, 0))]
    out_shape = [jax.ShapeDtypeStruct((wq_t.shape[0], n), BF16),
                 jax.ShapeDtypeStruct((n, K_AUG), BF16),
                 jax.ShapeDtypeStruct((n // MOBA_BLOCK, KV_DIM, MOBA_BLOCK), BF16)]
    if with_kmean:
        out_specs.append(pl.BlockSpec((nb, 1, K_AUG), lambda i: (i, 0, 0)))
        out_shape.append(jax.ShapeDtypeStruct((n // MOBA_BLOCK, 1, K_AUG), F32))
    return pl.pallas_call(
        functools.partial(_attn_proj_kernel, shared_norm=shared_norm),
        grid=(n // tm,),
        in_specs=[pl.BlockSpec((tm, d), lambda i: (i, 0)), gain, gain,
                  whole(wq_t), whole(wk_aug), whole(wv_t),
                  pl.BlockSpec((tm, PAIR_W), lambda i: (i % tiles_per_seq, 0))],
        out_specs=out_specs,
        out_shape=out_shape,
        compiler_params=_cparams(("parallel",)),
        name=name,
    )(x, gq.reshape(1, d), gkv.reshape(1, d), wq_t, wk_aug, wv_t, key_feat)


def _oproj_kernel(mix_ref, x_ref, g_ref, w_ref, o_ref):
    a = jnp.dot(mix_ref[...], w_ref[...], preferred_element_type=F32)
    o_ref[...] = x_ref[...] + (a * _rms_scale(a)) * g_ref[...]


def oproj_norm_residual(mix, x, g, w, *, tm, name):
    n, d = x.shape
    return pl.pallas_call(
        _oproj_kernel,
        grid=(n // tm,),
        in_specs=[pl.BlockSpec((tm, mix.shape[1]), lambda i: (i, 0)),
                  pl.BlockSpec((tm, d), lambda i: (i, 0)),
                  pl.BlockSpec((1, d), lambda i: (0, 0)),
                  pl.BlockSpec(w.shape, lambda i: (0, 0))],
        out_specs=pl.BlockSpec((tm, d), lambda i: (i, 0)),
        out_shape=jax.ShapeDtypeStruct((n, d), F32),
        compiler_params=_cparams(("parallel",)),
        name=name,
    )(mix, x, g.reshape(1, d), w)


MLP_FF_CHUNK = 512


def _mlp_kernel(x_ref, gpre_ref, gpost_ref, wup_ref, wdn_ref, o_ref, h_ref, acc_ref):
    f = pl.program_id(1)

    @pl.when(f == 0)
    def _():
        x = x_ref[...]
        h_ref[...] = ((x * _rms_scale(x)) * gpre_ref[...]).astype(BF16)
        acc_ref[...] = jnp.zeros(acc_ref.shape, F32)

    h = h_ref[...]
    chunks = wup_ref.shape[1] // MLP_FF_CHUNK

    def up(c):
        cols = slice(c * MLP_FF_CHUNK, (c + 1) * MLP_FF_CHUNK)
        return jnp.dot(h, wup_ref[:, cols], preferred_element_type=F32)

    u_next = up(0)
    for c in range(chunks):
        u = jnp.maximum(u_next, 0.0)
        if c + 1 < chunks:
            u_next = up(c + 1)
        rows = slice(c * MLP_FF_CHUNK, (c + 1) * MLP_FF_CHUNK)
        acc_ref[...] += jnp.dot((u * u).astype(BF16), wdn_ref[rows, :], preferred_element_type=F32)

    @pl.when(f == pl.num_programs(1) - 1)
    def _():
        a = acc_ref[...]
        o_ref[...] = x_ref[...] + (a * _rms_scale(a)) * gpost_ref[...]


def mlp_block(x, gpre, gpost, wup_all, wdn_all, layer, *, tm, tf, name):
    n, d = x.shape
    dff = wup_all.shape[2]
    return pl.pallas_call(
        _mlp_kernel,
        grid=(n // tm, dff // tf),
        in_specs=[pl.BlockSpec((tm, d), lambda i, f: (i, 0)),
                  pl.BlockSpec((1, d), lambda i, f: (0, 0)),
                  pl.BlockSpec((1, d), lambda i, f: (0, 0)),
                  pl.BlockSpec((None, d, tf), lambda i, f: (layer, 0, f)),
                  pl.BlockSpec((None, tf, d), lambda i, f: (layer, f, 0))],
        out_specs=pl.BlockSpec((tm, d), lambda i, f: (i, 0)),
        out_shape=jax.ShapeDtypeStruct((n, d), F32),
        scratch_shapes=[pltpu.VMEM((tm, d), BF16), pltpu.VMEM((tm, d), F32)],
        compiler_params=_cparams(("parallel", "arbitrary")),
        name=name,
    )(x, gpre.reshape(1, d), gpost.reshape(1, d), wup_all, wdn_all)


def _store_pairs(o_ref, out_t, width):
    for p in range(PAIRS):
        pair = jnp.concatenate([out_t[:, (2 * p) * width:(2 * p + 1) * width],
                                out_t[:, (2 * p + 1) * width:(2 * p + 2) * width]], axis=0)
        o_ref[:, p * PAIR_W:(p + 1) * PAIR_W] = pair.T.astype(o_ref.dtype)


def _swa_kernel(slope_ref, sink_ref, q_ref, k_ref, vt_ref, o_ref, lhs_ref):
    j = pl.program_id(2)
    half_lanes = GROUP * WINDOW
    qt = q_ref[...]
    feat = jnp.concatenate(
        [jnp.zeros((SEL_LANES, half_lanes), F32), slope_ref[0],
         jnp.zeros((HEAD_DIM - SEL_LANES - SLOPE_ROWS, half_lanes), F32)], axis=0).astype(BF16)
    for half in range(2):
        for h in range(GROUP):
            lhs_ref[half, 0:HEAD_DIM, h * WINDOW:(h + 1) * WINDOW] = (
                qt[h * HEAD_DIM:(h + 1) * HEAD_DIM, half * WINDOW:(half + 1) * WINDOW])
        lhs_ref[half, HEAD_DIM:2 * HEAD_DIM, :] = feat

    base = [jnp.maximum(j * MOBA_BLOCK - WINDOW, 0), j * MOBA_BLOCK]
    keys = [k_ref[pl.ds(pl.multiple_of(b, WINDOW), MOBA_BLOCK), :] for b in base]
    v_own = vt_ref[j]
    v_prev = vt_ref[jnp.maximum(j - 1, 0)]
    v_straddle = jnp.concatenate([v_prev[:, WINDOW:], v_own[:, :WINDOW]], axis=1)
    ones_rows = _ones_rows()
    vals = [jnp.concatenate([jnp.where(j > 0, v_straddle, v_own), ones_rows], axis=0),
            jnp.concatenate([v_own, ones_rows], axis=0)]

    key = lax.broadcasted_iota(jnp.int32, (MOBA_BLOCK, 2 * WINDOW), 0)
    qry = lax.broadcasted_iota(jnp.int32, (MOBA_BLOCK, 2 * WINDOW), 1) & (WINDOW - 1)
    keep = []
    for half in range(2):
        dist = (j * MOBA_BLOCK + half * WINDOW - base[half]) + qry - key
        keep.append((dist & -WINDOW) == 0)

    def scores(item):
        half, c = item
        return jnp.dot(keys[half], lhs_ref[half, :, c * 2 * WINDOW:(c + 1) * 2 * WINDOW],
                       preferred_element_type=F32)

    items = [(half, c) for c in range(PAIRS) for half in range(2)]
    raw, pvs, outs = {}, {}, {}

    slope = slope_ref[0, 0:1, :] + slope_ref[0, 1:2, :] + slope_ref[0, 2:3, :]
    lane_q = lax.broadcasted_iota(jnp.int32, (1, half_lanes), 1) & (WINDOW - 1)
    sinks = [sink_ref[0] + slope * (j * MOBA_BLOCK + half * WINDOW + lane_q).astype(F32)
             for half in range(2)]

    def softmax_pv(i):
        half, c = items[i]
        s = jnp.where(keep[half], raw.pop(i), NEG_INF)
        sink = sinks[half][:, c * 2 * WINDOW:(c + 1) * 2 * WINDOW]
        m = jnp.maximum(jnp.max(s, axis=0, keepdims=True), sink)
        e = jnp.exp2(s - m).astype(BF16)
        pv = jnp.dot(vals[half], e, preferred_element_type=F32)
        pvs[i] = (pv, jnp.exp2(sink - m))

    def normalise(i):
        pv, sink_term = pvs.pop(i)
        l = pv[HEAD_DIM:HEAD_DIM + 1, :] + sink_term
        outs[items[i]] = pv[0:HEAD_DIM, :] * (1.0 / l)

    for i in range(len(items) + 2):
        if i < len(items):
            raw[i] = scores(items[i])
        if 0 <= i - 1 < len(items):
            softmax_pv(i - 1)
        if 0 <= i - 2 < len(items):
            normalise(i - 2)

    for p in range(PAIRS):
        pair = jnp.concatenate(
            [jnp.concatenate([outs[(0, p)][:, :WINDOW], outs[(1, p)][:, :WINDOW]], axis=1),
             jnp.concatenate([outs[(0, p)][:, WINDOW:], outs[(1, p)][:, WINDOW:]], axis=1)],
            axis=0)
        o_ref[:, p * PAIR_W:(p + 1) * PAIR_W] = pair.T.astype(o_ref.dtype)


def swa_attention(q_t, k_aug, v_t, slope_rows, sink_rows, *, batch, seq):
    n = q_t.shape[1]
    nblk = seq // MOBA_BLOCK
    half_lanes = GROUP * WINDOW
    return pl.pallas_call(
        _swa_kernel,
        grid=(batch, N_KV_HEADS, nblk),
        in_specs=[pl.BlockSpec((1, SLOPE_ROWS, half_lanes), lambda b, kh, j: (kh, 0, 0)),
                  pl.BlockSpec((1, 1, half_lanes), lambda b, kh, j: (kh, 0, 0)),
                  pl.BlockSpec((GROUP * HEAD_DIM, MOBA_BLOCK), lambda b, kh, j: (kh, b * nblk + j)),
                  pl.BlockSpec((seq, PAIR_W), lambda b, kh, j: (b, kh)),
                  pl.BlockSpec((nblk, HEAD_DIM, MOBA_BLOCK), lambda b, kh, j: (b, kh, 0))],
        out_specs=pl.BlockSpec((MOBA_BLOCK, GROUP * HEAD_DIM), lambda b, kh, j: (b * nblk + j, kh)),
        out_shape=jax.ShapeDtypeStruct((n, Q_DIM), BF16),
        scratch_shapes=[pltpu.VMEM((2, 2 * HEAD_DIM, half_lanes), BF16)],
        compiler_params=_cparams(("parallel", "parallel", "arbitrary")),
        name="swa_attention",
    )(slope_rows, sink_rows, q_t, k_aug, v_t)


def _moba_kernel(slope_ref, q_ref, k_ref, vt_ref, km_ref, o_ref,
                 lhs_ref, acc_ref, m_ref, s0_ref, s1_ref):
    j = pl.program_id(2)
    lanes = GROUP * MOBA_BLOCK
    nblk = km_ref.shape[1]
    qt = q_ref[...]

    km = km_ref[0][:, :HEAD_DIM]
    gate = jnp.concatenate(
        [jnp.dot(km, qt[h * HEAD_DIM:(h + 1) * HEAD_DIM, :], preferred_element_type=F32)
         for h in range(GROUP)], axis=1)
    blk = lax.broadcasted_iota(jnp.int32, (nblk, lanes), 0)
    blkf = blk.astype(F32)
    g = jnp.where(blk < j, gate, NEG_INF)
    sel = jnp.zeros((nblk, lanes), F32)
    for _ in range(MOBA_TOPK):
        mx = jnp.max(g, axis=0, keepdims=True)
        idx = jnp.min(jnp.where(g == mx, blkf, float(nblk)), axis=0, keepdims=True)
        hit = blkf == idx
        sel = jnp.where(hit, 1.0, sel)
        g = jnp.where(hit, BELOW_NEG_INF, g)
    not_sel = jnp.where(blk < j, 1.0 - sel, jnp.where(blk == j, 0.0, 1.0))
    pad = jnp.zeros((HEAD_DIM - SEL_LANES - SLOPE_ROWS, lanes), F32)
    feat = jnp.concatenate([not_sel, slope_ref[0], pad], axis=0).astype(BF16)
    for h in range(GROUP):
        lhs_ref[0:HEAD_DIM, h * MOBA_BLOCK:(h + 1) * MOBA_BLOCK] = qt[h * HEAD_DIM:(h + 1) * HEAD_DIM, :]
    lhs_ref[HEAD_DIM:2 * HEAD_DIM, :] = feat

    ones_rows = _ones_rows()

    def head_cols(h):
        return slice(h * MOBA_BLOCK, (h + 1) * MOBA_BLOCK)

    def key_block(n):
        return k_ref[pl.ds(pl.multiple_of(n * MOBA_BLOCK, MOBA_BLOCK), MOBA_BLOCK), :]

    def scores(kblk, h):
        return jnp.dot(kblk, lhs_ref[:, head_cols(h)], preferred_element_type=F32)

    def consume(src_ref, v_aug, h):
        cols = head_cols(h)
        s = src_ref[:, cols]
        m_old = m_ref[:, cols]
        m_new = jnp.maximum(m_old, jnp.max(s, axis=0, keepdims=True))
        alpha = jnp.exp2(m_old - m_new)
        m_ref[:, cols] = m_new
        e = jnp.exp2(s - m_new).astype(BF16)
        pv = jnp.dot(v_aug, e, preferred_element_type=F32)
        acc_ref[:, cols] = acc_ref[:, cols] * alpha + pv

    def values(n):
        return jnp.concatenate([vt_ref[n], ones_rows], axis=0)

    def step(t, src_ref, dst_ref):
        kblk = key_block(t)
        v_aug = values(jnp.where(t == 0, j, t - 1))
        for h in range(GROUP):
            dst_ref[:, head_cols(h)] = scores(kblk, h)
            consume(src_ref, v_aug, h)

    m_ref[...] = jnp.full(m_ref.shape, RUNNING_MAX_INIT, F32)
    acc_ref[...] = jnp.zeros(acc_ref.shape, F32)
    key = lax.broadcasted_iota(jnp.int32, (MOBA_BLOCK, MOBA_BLOCK), 0)
    qry = lax.broadcasted_iota(jnp.int32, (MOBA_BLOCK, MOBA_BLOCK), 1)
    k_own = key_block(j)
    for h in range(GROUP):
        s0_ref[:, head_cols(h)] = jnp.where(key <= qry, scores(k_own, h), NEG_INF)

    def two_steps(i, carry):
        step(2 * i, s0_ref, s1_ref)
        step(2 * i + 1, s1_ref, s0_ref)
        return carry

    lax.fori_loop(0, j // 2, two_steps, 0)

    def last(src_ref):
        v_aug = values(jnp.where(j == 0, j, j - 1))
        for h in range(GROUP):
            consume(src_ref, v_aug, h)

    @pl.when(j % 2 == 1)
    def _():
        step(j - 1, s0_ref, s1_ref)
        last(s1_ref)

    @pl.when(j % 2 == 0)
    def _():
        last(s0_ref)

    out = acc_ref[0:HEAD_DIM, :] * (1.0 / acc_ref[HEAD_DIM:HEAD_DIM + 1, :])
    _store_pairs(o_ref, out, MOBA_BLOCK)


def moba_attention(q_t, k_aug, v_t, kmean, slope_rows, *, batch, seq):
    n = q_t.shape[1]
    nblk = seq // MOBA_BLOCK
    lanes = GROUP * MOBA_BLOCK
    return pl.pallas_call(
        _moba_kernel,
        grid=(batch, N_KV_HEADS, nblk),
        in_specs=[pl.BlockSpec((1, SLOPE_ROWS, lanes), lambda b, kh, j: (kh, 0, 0)),
                  pl.BlockSpec((GROUP * HEAD_DIM, MOBA_BLOCK), lambda b, kh, j: (kh, b * nblk + j)),
                  pl.BlockSpec((seq, PAIR_W), lambda b, kh, j: (b, kh)),
                  pl.BlockSpec((nblk, HEAD_DIM, MOBA_BLOCK), lambda b, kh, j: (b, kh, 0)),
                  pl.BlockSpec((1, nblk, PAIR_W), lambda b, kh, j: (b, 0, kh))],
        out_specs=pl.BlockSpec((MOBA_BLOCK, GROUP * HEAD_DIM), lambda b, kh, j: (b * nblk + j, kh)),
        out_shape=jax.ShapeDtypeStruct((n, Q_DIM), BF16),
        scratch_shapes=[pltpu.VMEM((2 * HEAD_DIM, lanes), BF16),
                        pltpu.VMEM((HEAD_DIM + ONES_ROWS, lanes), F32),
                        pltpu.VMEM((1, lanes), F32),
                        pltpu.VMEM((MOBA_BLOCK, lanes), F32),
                        pltpu.VMEM((MOBA_BLOCK, lanes), F32)],
        compiler_params=_cparams(("parallel", "parallel", "arbitrary")),
        name="moba_attention",
    )(slope_rows, q_t, k_aug, v_t, kmean)


def _pad_heads(w):
    d = w.shape[0]
    w4 = w.reshape(d, N_KV_HEADS, HEAD_DIM)
    return jnp.concatenate([w4, jnp.zeros_like(w4)], axis=-1).reshape(d, K_AUG)


def _alibi_slopes_base2():
    slopes = np.exp2(-8.0 * np.arange(1, N_HEADS + 1, dtype=np.float32) / N_HEADS)
    return (slopes.astype(np.float32) * np.float32(LOG2E)).astype(np.float32)


def _slope_rows(slopes, width):
    hi = slopes.astype(BF16).astype(np.float32)
    mid = (slopes - hi).astype(BF16).astype(np.float32)
    lo = (slopes - hi - mid).astype(BF16).astype(np.float32)
    pieces = np.stack([hi, mid, lo], axis=-1)
    pad = np.zeros((N_HEADS, SLOPE_ROWS - 2 * SLOPE_PIECES), np.float32)
    rows = np.concatenate([pieces, pieces, pad], axis=1)
    rows = rows.reshape(N_KV_HEADS, GROUP, SLOPE_ROWS).transpose(0, 2, 1)
    return np.repeat(rows, width, axis=2)


def _key_features(seq):
    pos = np.arange(seq, dtype=np.int32)[:, None]
    blk = pos // MOBA_BLOCK
    lane = np.arange(PAIR_W, dtype=np.int32)[None, :] - HEAD_DIM
    feat = np.where(lane == blk, np.float32(NEG_INF), np.float32(0.0))
    feat = np.where((lane >= SLOPE_BLK_LANE) & (lane < SLOPE_POS_LANE),
                    (blk * MOBA_BLOCK).astype(np.float32), feat)
    feat = np.where((lane >= SLOPE_POS_LANE) & (lane < SLOPE_POS_LANE + SLOPE_PIECES),
                    (pos % MOBA_BLOCK).astype(np.float32), feat)
    return np.where(lane < 0, np.float32(0.0), feat).astype(np.float32)


def kernel(x, w_qkv_a, sinks_a, w_o_a, kv_norm, w_kv_shared, w_q_b, w_o_b,
           norm_attn_pre, norm_attn_post, norm_mlp_pre, norm_mlp_post, w_up, w_down):
    batch, seq, d = x.shape
    n = batch * seq
    assert seq // MOBA_BLOCK == SEL_LANES and seq % MOBA_BLOCK == 0
    qscale = HEAD_DIM ** -0.5 * LOG2E
    slopes2 = _alibi_slopes_base2()
    key_feat = _key_features(seq)

    wq_a, wk_a, wv_a = jnp.split(w_qkv_a[0], [Q_DIM, Q_DIM + KV_DIM], axis=-1)
    wk_s, wv_s = jnp.split(w_kv_shared, [KV_DIM], axis=-1)
    wup_all = w_up.astype(BF16)
    wdn_all = w_down.astype(BF16)

    xf = x.reshape(n, d)

    q0_t, k0, v0_t = attn_proj(
        xf, norm_attn_pre[0], norm_attn_pre[0], (wq_a * qscale).T.astype(BF16),
        _pad_heads(wk_a).astype(BF16), wv_a.T.astype(BF16), key_feat,
        tm=512, seq=seq, shared_norm=True, with_kmean=False, name="attn_proj0")
    sink_rows = jnp.repeat((sinks_a[0].astype(F32) * LOG2E).reshape(N_KV_HEADS, 1, GROUP), WINDOW, axis=2)
    mix0 = swa_attention(q0_t, k0, v0_t, _slope_rows(slopes2, WINDOW), sink_rows, batch=batch, seq=seq)
    x1 = oproj_norm_residual(mix0, xf, norm_attn_post[0], w_o_a[0].astype(BF16), tm=512, name="oproj0")
    x2 = mlp_block(x1, norm_mlp_pre[0], norm_mlp_post[0], wup_all, wdn_all, 0, tm=512, tf=1024, name="mlp0")

    q1_t, k1, v1_t, kmean = attn_proj(
        x2, norm_attn_pre[1], kv_norm, (w_q_b[0] * qscale).T.astype(BF16),
        _pad_heads(wk_s).astype(BF16), wv_s.T.astype(BF16), key_feat,
        tm=512, seq=seq, shared_norm=False, with_kmean=True, name="attn_proj1")
    kmean = kmean.reshape(batch, seq // MOBA_BLOCK, K_AUG).astype(BF16)
    mix1 = moba_attention(q1_t, k1, v1_t, kmean, _slope_rows(slopes2, MOBA_BLOCK), batch=batch, seq=seq)
    x3 = oproj_norm_residual(mix1, x2, norm_attn_post[1], w_o_b[0].astype(BF16), tm=512, name="oproj1")
    x4 = mlp_block(x3, norm_mlp_pre[1], norm_mlp_post[1], wup_all, wdn_all, 1, tm=512, tf=1024, name="mlp1")
    return x4.reshape(batch, seq, d)
```

```python
import functools

import jax
import jax.numpy as jnp
import numpy as np
from jax import lax
from jax.experimental import pallas as pl
from jax.experimental.pallas import tpu as pltpu

D_MODEL = 2048
N_HEADS = 32
N_KV_HEADS = 4
HEAD_DIM = 64
GROUP = N_HEADS // N_KV_HEADS
PAIRS = GROUP // 2
PAIR_W = 2 * HEAD_DIM
Q_DIM = N_HEADS * HEAD_DIM
KV_DIM = N_KV_HEADS * HEAD_DIM
K_AUG = N_KV_HEADS * PAIR_W
D_FF = 4 * D_MODEL
WINDOW = 128
MOBA_BLOCK = 256
MOBA_TOPK = 3
RMS_EPS = 1e-6
NEG_INF = -1e30
BELOW_NEG_INF = -3e38
RUNNING_MAX_INIT = -1e38
LOG2E = 1.4426950408889634

SEL_LANES = 16
SLOPE_BLK_LANE = 16
SLOPE_POS_LANE = 19
SLOPE_PIECES = 3
SLOPE_ROWS = 8
ONES_ROWS = 16

VMEM_LIMIT = 56 * 1024 * 1024

BF16 = jnp.bfloat16
F32 = jnp.float32
NT_DIMS = (((1,), (1,)), ((), ()))


def _rms_scale(x):
    return lax.rsqrt(jnp.mean(x * x, axis=-1, keepdims=True) + RMS_EPS)


def _cparams(semantics):
    return pltpu.CompilerParams(dimension_semantics=semantics,
                                vmem_limit_bytes=VMEM_LIMIT)


def _ones_rows():
    row = lax.broadcasted_iota(jnp.int32, (ONES_ROWS, MOBA_BLOCK), 0)
    return jnp.where(row == 0, 1.0, 0.0).astype(BF16)


PROJ_CHUNK = 512


def _attn_proj_kernel(x_ref, gq_ref, gkv_ref, wqt_ref, wk_ref, wvt_ref, feat_ref,
                      qt_ref, k_ref, vt_ref, *km_refs, shared_norm):
    x = x_ref[...]
    xn = x * _rms_scale(x)
    hq = (xn * gq_ref[...]).astype(BF16)
    hkv = hq if shared_norm else (xn * gkv_ref[...]).astype(BF16)
    for c in range(0, wqt_ref.shape[0], PROJ_CHUNK):
        qt_ref[c:c + PROJ_CHUNK, :] = lax.dot_general(
            wqt_ref[c:c + PROJ_CHUNK, :], hq, NT_DIMS, preferred_element_type=F32).astype(BF16)
    nblk = x_ref.shape[0] // MOBA_BLOCK
    kacc = jnp.dot(hkv, wk_ref[...], preferred_element_type=F32)
    for km_ref in km_refs:
        for blk in range(nblk):
            rows = kacc[blk * MOBA_BLOCK:(blk + 1) * MOBA_BLOCK]
            km_ref[blk] = jnp.sum(rows, axis=0, keepdims=True) * (1.0 / MOBA_BLOCK)
    feat = feat_ref[...]
    for kh in range(N_KV_HEADS):
        cols = slice(kh * PAIR_W, (kh + 1) * PAIR_W)
        k_ref[:, cols] = (kacc[:, cols] + feat).astype(BF16)
    vt = lax.dot_general(wvt_ref[...], hkv, NT_DIMS, preferred_element_type=F32)
    for blk in range(nblk):
        vt_ref[blk] = vt[:, blk * MOBA_BLOCK:(blk + 1) * MOBA_BLOCK].astype(BF16)


def attn_proj(x, gq, gkv, wq_t, wk_aug, wv_t, key_feat, *, tm, seq, shared_norm, with_kmean, name):
    n, d = x.shape
    nb = tm // MOBA_BLOCK
    tiles_per_seq = seq // tm
    whole = lambda w: pl.BlockSpec(w.shape, lambda i: (0, 0))
    gain = pl.BlockSpec((1, d), lambda i: (0, 0))
    out_specs = [pl.BlockSpec((wq_t.shape[0], tm), lambda i: (0, i)),
                 pl.BlockSpec((tm, K_AUG), lambda i: (i, 0)),
                 pl.BlockSpec((nb, KV_DIM, MOBA_BLOCK), lambda i: (i, 0, 0))]
    out_shape = [jax.ShapeDtypeStruct((wq_t.shape[0], n), BF16),
                 jax.ShapeDtypeStruct((n, K_AUG), BF16),
                 jax.ShapeDtypeStruct((n // MOBA_BLOCK, KV_DIM, MOBA_BLOCK), BF16)]
    if with_kmean:
        out_specs.append(pl.BlockSpec((nb, 1, K_AUG), lambda i: (i, 0, 0)))
        out_shape.append(jax.ShapeDtypeStruct((n // MOBA_BLOCK, 1, K_AUG), F32))
    return pl.pallas_call(
        functools.partial(_attn_proj_kernel, shared_norm=shared_norm),
        grid=(n // tm,),
        in_specs=[pl.BlockSpec((tm, d), lambda i: (i, 0)), gain, gain,
                  whole(wq_t), whole(wk_aug), whole(wv_t),
                  pl.BlockSpec((tm, PAIR_W), lambda i: (i % tiles_per_seq, 0))],
        out_specs=out_specs,
        out_shape=out_shape,
        compiler_params=_cparams(("parallel",)),
        name=name,
    )(x, gq.reshape(1, d), gkv.reshape(1, d), wq_t, wk_aug, wv_t, key_feat)


def _oproj_kernel(mix_ref, x_ref, g_ref, w_ref, o_ref):
    a = jnp.dot(mix_ref[...], w_ref[...], preferred_element_type=F32)
    o_ref[...] = x_ref[...] + (a * _rms_scale(a)) * g_ref[...]


def oproj_norm_residual(mix, x, g, w, *, tm, name):
    n, d = x.shape
    return pl.pallas_call(
        _oproj_kernel,
        grid=(n // tm,),
        in_specs=[pl.BlockSpec((tm, mix.shape[1]), lambda i: (i, 0)),
                  pl.BlockSpec((tm, d), lambda i: (i, 0)),
                  pl.BlockSpec((1, d), lambda i: (0, 0)),
                  pl.BlockSpec(w.shape, lambda i: (0, 0))],
        out_specs=pl.BlockSpec((tm, d), lambda i: (i, 0)),
        out_shape=jax.ShapeDtypeStruct((n, d), F32),
        compiler_params=_cparams(("parallel",)),
        name=name,
    )(mix, x, g.reshape(1, d), w)


MLP_FF_CHUNK = 512


def _mlp_kernel(x_ref, gpre_ref, gpost_ref, wup_ref, wdn_ref, o_ref, h_ref, acc_ref):
    f = pl.program_id(1)

    @pl.when(f == 0)
    def _():
        x = x_ref[...]
        h_ref[...] = ((x * _rms_scale(x)) * gpre_ref[...]).astype(BF16)
        acc_ref[...] = jnp.zeros(acc_ref.shape, F32)

    h = h_ref[...]
    chunks = wup_ref.shape[1] // MLP_FF_CHUNK

    def up(c):
        cols = slice(c * MLP_FF_CHUNK, (c + 1) * MLP_FF_CHUNK)
        return jnp.dot(h, wup_ref[:, cols], preferred_element_type=F32)

    u_next = up(0)
    for c in range(chunks):
        u = jnp.maximum(u_next, 0.0)
        if c + 1 < chunks:
            u_next = up(c + 1)
        rows = slice(c * MLP_FF_CHUNK, (c + 1) * MLP_FF_CHUNK)
        acc_ref[...] += jnp.dot((u * u).astype(BF16), wdn_ref[rows, :], preferred_element_type=F32)

    @pl.when(f == pl.num_programs(1) - 1)
    def _():
        a = acc_ref[...]
        o_ref[...] = x_ref[...] + (a * _rms_scale(a)) * gpost_ref[...]


def mlp_block(x, gpre, gpost, wup_all, wdn_all, layer, *, tm, tf, name):
    n, d = x.shape
    dff = wup_all.shape[2]
    return pl.pallas_call(
        _mlp_kernel,
        grid=(n // tm, dff // tf),
        in_specs=[pl.BlockSpec((tm, d), lambda i, f: (i, 0)),
                  pl.BlockSpec((1, d), lambda i, f: (0, 0)),
                  pl.BlockSpec((1, d), lambda i, f: (0, 0)),
                  pl.BlockSpec((None, d, tf), lambda i, f: (layer, 0, f)),
                  pl.BlockSpec((None, tf, d), lambda i, f: (layer, f, 0))],
        out_specs=pl.BlockSpec((tm, d), lambda i, f: (i, 0)),
        out_shape=jax.ShapeDtypeStruct((n, d), F32),
        scratch_shapes=[pltpu.VMEM((tm, d), BF16), pltpu.VMEM((tm, d), F32)],
        compiler_params=_cparams(("parallel", "arbitrary")),
        name=name,
    )(x, gpre.reshape(1, d), gpost.reshape(1, d), wup_all, wdn_all)


def _store_pairs(o_ref, out_t, width):
    for p in range(PAIRS):
        pair = jnp.concatenate([out_t[:, (2 * p) * width:(2 * p + 1) * width],
                                out_t[:, (2 * p + 1) * width:(2 * p + 2) * width]], axis=0)
        o_ref[:, p * PAIR_W:(p + 1) * PAIR_W] = pair.T.astype(o_ref.dtype)


SWA_PV_LAG = 6


def _swa_kernel(slope_ref, sink_ref, q_ref, k_ref, vt_ref, o_ref, lhs_ref):
    j = pl.program_id(1)
    half_lanes = GROUP * WINDOW
    zeros_top = jnp.zeros((SEL_LANES, half_lanes), F32)
    zeros_bottom = jnp.zeros((HEAD_DIM - SEL_LANES - SLOPE_ROWS, half_lanes), F32)
    for kh in range(N_KV_HEADS):
        feat = jnp.concatenate([zeros_top, slope_ref[kh], zeros_bottom], axis=0).astype(BF16)
        for half in range(2):
            for h in range(GROUP):
                row0 = (kh * GROUP + h) * HEAD_DIM
                lhs_ref[kh, half, 0:HEAD_DIM, h * WINDOW:(h + 1) * WINDOW] = (
                    q_ref[row0:row0 + HEAD_DIM, half * WINDOW:(half + 1) * WINDOW])
            lhs_ref[kh, half, HEAD_DIM:2 * HEAD_DIM, :] = feat

    base = [jnp.maximum(j * MOBA_BLOCK - WINDOW, 0), j * MOBA_BLOCK]
    ones_rows = _ones_rows()
    v_own = vt_ref[j]
    v_prev = vt_ref[jnp.maximum(j - 1, 0)]
    v_straddle = jnp.concatenate([v_prev[:, WINDOW:], v_own[:, :WINDOW]], axis=1)
    v_half = [jnp.where(j > 0, v_straddle, v_own), v_own]

    def keys(kh, half):
        return k_ref[pl.ds(pl.multiple_of(base[half], WINDOW), MOBA_BLOCK), kh * PAIR_W:(kh + 1) * PAIR_W]

    def values(kh, half):
        return jnp.concatenate([v_half[half][kh * HEAD_DIM:(kh + 1) * HEAD_DIM, :], ones_rows], axis=0)

    key = lax.broadcasted_iota(jnp.int32, (MOBA_BLOCK, 2 * WINDOW), 0)
    qry = lax.broadcasted_iota(jnp.int32, (MOBA_BLOCK, 2 * WINDOW), 1) & (WINDOW - 1)
    keep = []
    for half in range(2):
        dist = (j * MOBA_BLOCK + half * WINDOW - base[half]) + qry - key
        keep.append((dist & -WINDOW) == 0)

    lane_q = lax.broadcasted_iota(jnp.int32, (1, half_lanes), 1) & (WINDOW - 1)
    sinks = {}
    for kh in range(N_KV_HEADS):
        slope = slope_ref[kh, 0:1, :] + slope_ref[kh, 1:2, :] + slope_ref[kh, 2:3, :]
        for half in range(2):
            pos = (j * MOBA_BLOCK + half * WINDOW + lane_q).astype(F32)
            sinks[(kh, half)] = sink_ref[kh] + slope * pos

    items = [(kh, half, c) for kh in range(N_KV_HEADS) for c in range(PAIRS) for half in range(2)]
    raw, pvs = {}, {}

    def pair_cols(c):
        return slice(c * 2 * WINDOW, (c + 1) * 2 * WINDOW)

    def scores(i):
        kh, half, c = items[i]
        raw[i] = jnp.dot(keys(kh, half), lhs_ref[kh, half, :, pair_cols(c)],
                         preferred_element_type=F32)

    def softmax_pv(i):
        kh, half, c = items[i]
        s = jnp.where(keep[half], raw.pop(i), NEG_INF)
        sink = sinks[(kh, half)][:, pair_cols(c)]
        m = jnp.maximum(jnp.max(s, axis=0, keepdims=True), sink)
        e = jnp.exp2(s - m).astype(BF16)
        pv = jnp.dot(values(kh, half), e, preferred_element_type=F32)
        pvs[i] = (pv, jnp.exp2(sink - m))

    def normalise(i):
        pv, sink_term = pvs.pop(i)
        l = pv[HEAD_DIM:HEAD_DIM + 1, :] + sink_term
        return pv[0:HEAD_DIM, :] * (1.0 / l)

    done = {}
    for i in range(len(items) + SWA_PV_LAG + 1):
        if i < len(items):
            scores(i)
        if 0 <= i - SWA_PV_LAG < len(items):
            softmax_pv(i - SWA_PV_LAG)
        n = i - SWA_PV_LAG - 1
        if 0 <= n < len(items):
            kh, half, c = items[n]
            done[half] = normalise(n)
            if half == 1:
                pair = jnp.concatenate(
                    [jnp.concatenate([done[0][:, :WINDOW], done[1][:, :WINDOW]], axis=1),
                     jnp.concatenate([done[0][:, WINDOW:], done[1][:, WINDOW:]], axis=1)],
                    axis=0)
                col0 = (kh * PAIRS + c) * PAIR_W
                o_ref[:, col0:col0 + PAIR_W] = pair.T.astype(o_ref.dtype)


def swa_attention(q_t, k_aug, v_t, slope_rows, sink_rows, *, batch, seq):
    n = q_t.shape[1]
    nblk = seq // MOBA_BLOCK
    half_lanes = GROUP * WINDOW
    return pl.pallas_call(
        _swa_kernel,
        grid=(batch, nblk),
        in_specs=[pl.BlockSpec(slope_rows.shape, lambda b, j: (0, 0, 0)),
                  pl.BlockSpec(sink_rows.shape, lambda b, j: (0, 0, 0)),
                  pl.BlockSpec((Q_DIM, MOBA_BLOCK), lambda b, j: (0, b * nblk + j)),
                  pl.BlockSpec((seq, K_AUG), lambda b, j: (b, 0)),
                  pl.BlockSpec((nblk, KV_DIM, MOBA_BLOCK), lambda b, j: (b, 0, 0))],
        out_specs=pl.BlockSpec((MOBA_BLOCK, Q_DIM), lambda b, j: (b * nblk + j, 0)),
        out_shape=jax.ShapeDtypeStruct((n, Q_DIM), BF16),
        scratch_shapes=[pltpu.VMEM((N_KV_HEADS, 2, 2 * HEAD_DIM, half_lanes), BF16)],
        compiler_params=_cparams(("parallel", "arbitrary")),
        name="swa_attention",
    )(slope_rows, sink_rows, q_t, k_aug, v_t)


MOBA_PV_LAG = 2


def _moba_kernel(slope_ref, q_ref, k_ref, vt_ref, km_ref, o_ref,
                 lhs_ref, acc_ref, m_ref, s0_ref, s1_ref):
    j = pl.program_id(2)
    lanes = GROUP * MOBA_BLOCK
    nblk = km_ref.shape[1]
    qt = q_ref[...]

    km = km_ref[0][:, :HEAD_DIM]
    gate = jnp.concatenate(
        [jnp.dot(km, qt[h * HEAD_DIM:(h + 1) * HEAD_DIM, :], preferred_element_type=F32)
         for h in range(GROUP)], axis=1)
    blk = lax.broadcasted_iota(jnp.int32, (nblk, lanes), 0)
    blkf = blk.astype(F32)
    g = jnp.where(blk < j, gate, NEG_INF)
    sel = jnp.zeros((nblk, lanes), F32)
    for _ in range(MOBA_TOPK):
        mx = jnp.max(g, axis=0, keepdims=True)
        idx = jnp.min(jnp.where(g == mx, blkf, float(nblk)), axis=0, keepdims=True)
        hit = blkf == idx
        sel = jnp.where(hit, 1.0, sel)
        g = jnp.where(hit, BELOW_NEG_INF, g)
    not_sel = jnp.where(blk < j, 1.0 - sel, jnp.where(blk == j, 0.0, 1.0))
    pad = jnp.zeros((HEAD_DIM - SEL_LANES - SLOPE_ROWS, lanes), F32)
    feat = jnp.concatenate([not_sel, slope_ref[0], pad], axis=0).astype(BF16)
    for h in range(GROUP):
        lhs_ref[0:HEAD_DIM, h * MOBA_BLOCK:(h + 1) * MOBA_BLOCK] = qt[h * HEAD_DIM:(h + 1) * HEAD_DIM, :]
    lhs_ref[HEAD_DIM:2 * HEAD_DIM, :] = feat

    ones_rows = _ones_rows()

    def head_cols(h):
        return slice(h * MOBA_BLOCK, (h + 1) * MOBA_BLOCK)

    def key_block(n):
        return k_ref[pl.ds(pl.multiple_of(n * MOBA_BLOCK, MOBA_BLOCK), MOBA_BLOCK), :]

    def scores(kblk, h):
        return jnp.dot(kblk, lhs_ref[:, head_cols(h)], preferred_element_type=F32)

    def softmax(src_ref, h):
        cols = head_cols(h)
        s = src_ref[:, cols]
        m_old = m_ref[:, cols]
        m_new = jnp.maximum(m_old, jnp.max(s, axis=0, keepdims=True))
        m_ref[:, cols] = m_new
        return h, jnp.exp2(s - m_new).astype(BF16), jnp.exp2(m_old - m_new)

    def accumulate(v_aug, h, e, alpha):
        cols = head_cols(h)
        pv = jnp.dot(v_aug, e, preferred_element_type=F32)
        acc_ref[:, cols] = acc_ref[:, cols] * alpha + pv

    def consume(src_ref, v_aug, h):
        accumulate(v_aug, *softmax(src_ref, h))

    def values(n):
        return jnp.concatenate([vt_ref[n], ones_rows], axis=0)

    def step(t, src_ref, dst_ref):
        kblk = key_block(t)
        v_aug = values(jnp.where(t == 0, j, t - 1))
        pending = []
        for h in range(GROUP):
            dst_ref[:, head_cols(h)] = scores(kblk, h)
            pending.append(softmax(src_ref, h))
            if len(pending) > MOBA_PV_LAG:
                accumulate(v_aug, *pending.pop(0))
        for ready in pending:
            accumulate(v_aug, *ready)

    m_ref[...] = jnp.full(m_ref.shape, RUNNING_MAX_INIT, F32)
    acc_ref[...] = jnp.zeros(acc_ref.shape, F32)
    key = lax.broadcasted_iota(jnp.int32, (MOBA_BLOCK, MOBA_BLOCK), 0)
    qry = lax.broadcasted_iota(jnp.int32, (MOBA_BLOCK, MOBA_BLOCK), 1)
    k_own = key_block(j)
    for h in range(GROUP):
        s0_ref[:, head_cols(h)] = jnp.where(key <= qry, scores(k_own, h), NEG_INF)

    def two_steps(i, carry):
        step(2 * i, s0_ref, s1_ref)
        step(2 * i + 1, s1_ref, s0_ref)
        return carry

    lax.fori_loop(0, j // 2, two_steps, 0)

    def last(src_ref):
        v_aug = values(jnp.where(j == 0, j, j - 1))
        for h in range(GROUP):
            consume(src_ref, v_aug, h)

    @pl.when(j % 2 == 1)
    def _():
        step(j - 1, s0_ref, s1_ref)
        last(s1_ref)

    @pl.when(j % 2 == 0)
    def _():
        last(s0_ref)

    out = acc_ref[0:HEAD_DIM, :] * (1.0 / acc_ref[HEAD_DIM:HEAD_DIM + 1, :])
    _store_pairs(o_ref, out, MOBA_BLOCK)


def moba_attention(q_t, k_aug, v_t, kmean, slope_rows, *, batch, seq):
    n = q_t.shape[1]
    nblk = seq // MOBA_BLOCK
    lanes = GROUP * MOBA_BLOCK
    return pl.pallas_call(
        _moba_kernel,
        grid=(batch, N_KV_HEADS, nblk),
        in_specs=[pl.BlockSpec((1, SLOPE_ROWS, lanes), lambda b, kh, j: (kh, 0, 0)),
                  pl.BlockSpec((GROUP * HEAD_DIM, MOBA_BLOCK), lambda b, kh, j: (kh, b * nblk + j)),
                  pl.BlockSpec((seq, PAIR_W), lambda b, kh, j: (b, kh)),
                  pl.BlockSpec((nblk, HEAD_DIM, MOBA_BLOCK), lambda b, kh, j: (b, kh, 0)),
                  pl.BlockSpec((1, nblk, PAIR_W), lambda b, kh, j: (b, 0, kh))],
        out_specs=pl.BlockSpec((MOBA_BLOCK, GROUP * HEAD_DIM), lambda b, kh, j: (b * nblk + j, kh)),
        out_shape=jax.ShapeDtypeStruct((n, Q_DIM), BF16),
        scratch_shapes=[pltpu.VMEM((2 * HEAD_DIM, lanes), BF16),
                        pltpu.VMEM((HEAD_DIM + ONES_ROWS, lanes), F32),
                        pltpu.VMEM((1, lanes), F32),
                        pltpu.VMEM((MOBA_BLOCK, lanes), F32),
                        pltpu.VMEM((MOBA_BLOCK, lanes), F32)],
        compiler_params=_cparams(("parallel", "parallel", "arbitrary")),
        name="moba_attention",
    )(slope_rows, q_t, k_aug, v_t, kmean)


def _pad_heads(w):
    d = w.shape[0]
    w4 = w.reshape(d, N_KV_HEADS, HEAD_DIM)
    return jnp.concatenate([w4, jnp.zeros_like(w4)], axis=-1).reshape(d, K_AUG)


def _alibi_slopes_base2():
    slopes = np.exp2(-8.0 * np.arange(1, N_HEADS + 1, dtype=np.float32) / N_HEADS)
    return (slopes.astype(np.float32) * np.float32(LOG2E)).astype(np.float32)


def _slope_rows(slopes, width):
    hi = slopes.astype(BF16).astype(np.float32)
    mid = (slopes - hi).astype(BF16).astype(np.float32)
    lo = (slopes - hi - mid).astype(BF16).astype(np.float32)
    pieces = np.stack([hi, mid, lo], axis=-1)
    pad = np.zeros((N_HEADS, SLOPE_ROWS - 2 * SLOPE_PIECES), np.float32)
    rows = np.concatenate([pieces, pieces, pad], axis=1)
    rows = rows.reshape(N_KV_HEADS, GROUP, SLOPE_ROWS).transpose(0, 2, 1)
    return np.repeat(rows, width, axis=2)


def _key_features(seq):
    pos = np.arange(seq, dtype=np.int32)[:, None]
    blk = pos // MOBA_BLOCK
    lane = np.arange(PAIR_W, dtype=np.int32)[None, :] - HEAD_DIM
    feat = np.where(lane == blk, np.float32(NEG_INF), np.float32(0.0))
    feat = np.where((lane >= SLOPE_BLK_LANE) & (lane < SLOPE_POS_LANE),
                    (blk * MOBA_BLOCK).astype(np.float32), feat)
    feat = np.where((lane >= SLOPE_POS_LANE) & (lane < SLOPE_POS_LANE + SLOPE_PIECES),
                    (pos % MOBA_BLOCK).astype(np.float32), feat)
    return np.where(lane < 0, np.float32(0.0), feat).astype(np.float32)


def kernel(x, w_qkv_a, sinks_a, w_o_a, kv_norm, w_kv_shared, w_q_b, w_o_b,
           norm_attn_pre, norm_attn_post, norm_mlp_pre, norm_mlp_post, w_up, w_down):
    batch, seq, d = x.shape
    n = batch * seq
    assert seq // MOBA_BLOCK == SEL_LANES and seq % MOBA_BLOCK == 0
    qscale = HEAD_DIM ** -0.5 * LOG2E
    slopes2 = _alibi_slopes_base2()
    key_feat = _key_features(seq)

    wq_a, wk_a, wv_a = jnp.split(w_qkv_a[0], [Q_DIM, Q_DIM + KV_DIM], axis=-1)
    wk_s, wv_s = jnp.split(w_kv_shared, [KV_DIM], axis=-1)
    wup_all = w_up.astype(BF16)
    wdn_all = w_down.astype(BF16)

    xf = x.reshape(n, d)

    q0_t, k0, v0_t = attn_proj(
        xf, norm_attn_pre[0], norm_attn_pre[0], (wq_a * qscale).T.astype(BF16),
        _pad_heads(wk_a).astype(BF16), wv_a.T.astype(BF16), key_feat,
        tm=512, seq=seq, shared_norm=True, with_kmean=False, name="attn_proj0")
    sink_rows = jnp.repeat((sinks_a[0].astype(F32) * LOG2E).reshape(N_KV_HEADS, 1, GROUP), WINDOW, axis=2)
    mix0 = swa_attention(q0_t, k0, v0_t, _slope_rows(slopes2, WINDOW), sink_rows, batch=batch, seq=seq)
    x1 = oproj_norm_residual(mix0, xf, norm_attn_post[0], w_o_a[0].astype(BF16), tm=512, name="oproj0")
    x2 = mlp_block(x1, norm_mlp_pre[0], norm_mlp_post[0], wup_all, wdn_all, 0, tm=512, tf=1024, name="mlp0")

    q1_t, k1, v1_t, kmean = attn_proj(
        x2, norm_attn_pre[1], kv_norm, (w_q_b[0] * qscale).T.astype(BF16),
        _pad_heads(wk_s).astype(BF16), wv_s.T.astype(BF16), key_feat,
        tm=512, seq=seq, shared_norm=False, with_kmean=True, name="attn_proj1")
    kmean = kmean.reshape(batch, seq // MOBA_BLOCK, K_AUG).astype(BF16)
    mix1 = moba_attention(q1_t, k1, v1_t, kmean, _slope_rows(slopes2, MOBA_BLOCK), batch=batch, seq=seq)
    x3 = oproj_norm_residual(mix1, x2, norm_attn_post[1], w_o_b[0].astype(BF16), tm=512, name="oproj1")
    x4 = mlp_block(x3, norm_mlp_pre[1], norm_mlp_post[1], wup_all, wdn_all, 1, tm=512, tf=1024, name="mlp1")
    return x4.reshape(batch, seq, d)
```

```python
import functools

import jax
import jax.numpy as jnp
import numpy as np
from jax import lax
from jax.experimental import pallas as pl
from jax.experimental.pallas import tpu as pltpu

D_MODEL = 2048
N_HEADS = 32
N_KV_HEADS = 4
HEAD_DIM = 64
GROUP = N_HEADS // N_KV_HEADS
PAIRS = GROUP // 2
PAIR_W = 2 * HEAD_DIM
Q_DIM = N_HEADS * HEAD_DIM
KV_DIM = N_KV_HEADS * HEAD_DIM
K_AUG = N_KV_HEADS * PAIR_W
D_FF = 4 * D_MODEL
WINDOW = 128
MOBA_BLOCK = 256
MOBA_TOPK = 3
RMS_EPS = 1e-6
NEG_INF = -1e30
BELOW_NEG_INF = -3e38
RUNNING_MAX_INIT = -1e38
LOG2E = 1.4426950408889634

SEL_LANES = 16
SLOPE_BLK_LANE = 16
SLOPE_POS_LANE = 19
SLOPE_PIECES = 3
SLOPE_ROWS = 8
ONES_ROWS = 16

VMEM_LIMIT = 56 * 1024 * 1024

BF16 = jnp.bfloat16
F32 = jnp.float32
NT_DIMS = (((1,), (1,)), ((), ()))


def _rms_scale(x):
    return lax.rsqrt(jnp.mean(x * x, axis=-1, keepdims=True) + RMS_EPS)


def _cparams(semantics):
    return pltpu.CompilerParams(dimension_semantics=semantics,
                                vmem_limit_bytes=VMEM_LIMIT)


def _ones_rows():
    row = lax.broadcasted_iota(jnp.int32, (ONES_ROWS, MOBA_BLOCK), 0)
    return jnp.where(row == 0, 1.0, 0.0).astype(BF16)


PROJ_CHUNK = 512


def _attn_proj_kernel(x_ref, gq_ref, gkv_ref, wqt_ref, wk_ref, wvt_ref, feat_ref,
                      qt_ref, k_ref, vt_ref, *km_refs, shared_norm):
    x = x_ref[...]
    xn = x * _rms_scale(x)
    hq = (xn * gq_ref[...]).astype(BF16)
    hkv = hq if shared_norm else (xn * gkv_ref[...]).astype(BF16)
    for c in range(0, wqt_ref.shape[0], PROJ_CHUNK):
        qt_ref[c:c + PROJ_CHUNK, :] = lax.dot_general(
            wqt_ref[c:c + PROJ_CHUNK, :], hq, NT_DIMS, preferred_element_type=F32).astype(BF16)
    nblk = x_ref.shape[0] // MOBA_BLOCK
    kacc = jnp.dot(hkv, wk_ref[...], preferred_element_type=F32)
    for km_ref in km_refs:
        for blk in range(nblk):
            rows = kacc[blk * MOBA_BLOCK:(blk + 1) * MOBA_BLOCK]
            km_ref[blk] = jnp.sum(rows, axis=0, keepdims=True) * (1.0 / MOBA_BLOCK)
    feat = feat_ref[...]
    for kh in range(N_KV_HEADS):
        cols = slice(kh * PAIR_W, (kh + 1) * PAIR_W)
        k_ref[:, cols] = (kacc[:, cols] + feat).astype(BF16)
    vt = lax.dot_general(wvt_ref[...], hkv, NT_DIMS, preferred_element_type=F32)
    for blk in range(nblk):
        vt_ref[blk] = vt[:, blk * MOBA_BLOCK:(blk + 1) * MOBA_BLOCK].astype(BF16)


def attn_proj(x, gq, gkv, wq_t, wk_aug, wv_t, key_feat, *, tm, seq, shared_norm, with_kmean, name):
    n, d = x.shape
    nb = tm // MOBA_BLOCK
    tiles_per_seq = seq // tm
    whole = lambda w: pl.BlockSpec(w.shape, lambda i: (0, 0))
    gain = pl.BlockSpec((1, d), lambda i: (0, 0))
    out_specs = [pl.BlockSpec((wq_t.shape[0], tm), lambda i: (0, i)),
                 pl.BlockSpec((tm, K_AUG), lambda i: (i, 0)),
                 pl.BlockSpec((nb, KV_DIM, MOBA_BLOCK), lambda i: (i, 0, 0))]
    out_shape = [jax.ShapeDtypeStruct((wq_t.shape[0], n), BF16),
                 jax.ShapeDtypeStruct((n, K_AUG), BF16),
                 jax.ShapeDtypeStruct((n // MOBA_BLOCK, KV_DIM, MOBA_BLOCK), BF16)]
    if with_kmean:
        out_specs.append(pl.BlockSpec((nb, 1, K_AUG), lambda i: (i, 0, 0)))
        out_shape.append(jax.ShapeDtypeStruct((n // MOBA_BLOCK, 1, K_AUG), F32))
    return pl.pallas_call(
        functools.partial(_attn_proj_kernel, shared_norm=shared_norm),
        grid=(n // tm,),
        in_specs=[pl.BlockSpec((tm, d), lambda i: (i, 0)), gain, gain,
                  whole(wq_t), whole(wk_aug), whole(wv_t),
                  pl.BlockSpec((tm, PAIR_W), lambda i: (i % tiles_per_seq, 0))],
        out_specs=out_specs,
        out_shape=out_shape,
        compiler_params=_cparams(("parallel",)),
        name=name,
    )(x, gq.reshape(1, d), gkv.reshape(1, d), wq_t, wk_aug, wv_t, key_feat)


def _oproj_kernel(mix_ref, x_ref, g_ref, w_ref, o_ref):
    a = jnp.dot(mix_ref[...], w_ref[...], preferred_element_type=F32)
    o_ref[...] = x_ref[...] + (a * _rms_scale(a)) * g_ref[...]


def oproj_norm_residual(mix, x, g, w, *, tm, name):
    n, d = x.shape
    return pl.pallas_call(
        _oproj_kernel,
        grid=(n // tm,),
        in_specs=[pl.BlockSpec((tm, mix.shape[1]), lambda i: (i, 0)),
                  pl.BlockSpec((tm, d), lambda i: (i, 0)),
                  pl.BlockSpec((1, d), lambda i: (0, 0)),
                  pl.BlockSpec(w.shape, lambda i: (0, 0))],
        out_specs=pl.BlockSpec((tm, d), lambda i: (i, 0)),
        out_shape=jax.ShapeDtypeStruct((n, d), F32),
        compiler_params=_cparams(("parallel",)),
        name=name,
    )(mix, x, g.reshape(1, d), w)


MLP_FF_CHUNK = 512


def _mlp_kernel(x_ref, gpre_ref, gpost_ref, wup_ref, wdn_ref, o_ref, h_ref, acc_ref):
    f = pl.program_id(1)

    @pl.when(f == 0)
    def _():
        x = x_ref[...]
        h_ref[...] = ((x * _rms_scale(x)) * gpre_ref[...]).astype(BF16)
        acc_ref[...] = jnp.zeros(acc_ref.shape, F32)

    h = h_ref[...]
    chunks = wup_ref.shape[1] // MLP_FF_CHUNK

    def up(c):
        cols = slice(c * MLP_FF_CHUNK, (c + 1) * MLP_FF_CHUNK)
        return jnp.dot(h, wup_ref[:, cols], preferred_element_type=F32)

    u_next = up(0)
    for c in range(chunks):
        u = jnp.maximum(u_next, 0.0)
        if c + 1 < chunks:
            u_next = up(c + 1)
        rows = slice(c * MLP_FF_CHUNK, (c + 1) * MLP_FF_CHUNK)
        acc_ref[...] += jnp.dot((u * u).astype(BF16), wdn_ref[rows, :], preferred_element_type=F32)

    @pl.when(f == pl.num_programs(1) - 1)
    def _():
        a = acc_ref[...]
        o_ref[...] = x_ref[...] + (a * _rms_scale(a)) * gpost_ref[...]


def mlp_block(x, gpre, gpost, wup_all, wdn_all, layer, *, tm, tf, name):
    n, d = x.shape
    dff = wup_all.shape[2]
    return pl.pallas_call(
        _mlp_kernel,
        grid=(n // tm, dff // tf),
        in_specs=[pl.BlockSpec((tm, d), lambda i, f: (i, 0)),
                  pl.BlockSpec((1, d), lambda i, f: (0, 0)),
                  pl.BlockSpec((1, d), lambda i, f: (0, 0)),
                  pl.BlockSpec((None, d, tf), lambda i, f: (layer, 0, f)),
                  pl.BlockSpec((None, tf, d), lambda i, f: (layer, f, 0))],
        out_specs=pl.BlockSpec((tm, d), lambda i, f: (i, 0)),
        out_shape=jax.ShapeDtypeStruct((n, d), F32),
        scratch_shapes=[pltpu.VMEM((tm, d), BF16), pltpu.VMEM((tm, d), F32)],
        compiler_params=_cparams(("parallel", "arbitrary")),
        name=name,
    )(x, gpre.reshape(1, d), gpost.reshape(1, d), wup_all, wdn_all)


def _store_pairs(o_ref, out_t, width):
    for p in range(PAIRS):
        pair = jnp.concatenate([out_t[:, (2 * p) * width:(2 * p + 1) * width],
                                out_t[:, (2 * p + 1) * width:(2 * p + 2) * width]], axis=0)
        o_ref[:, p * PAIR_W:(p + 1) * PAIR_W] = pair.T.astype(o_ref.dtype)


SWA_PV_LAG = 6


def _swa_kernel(slope_ref, sink_ref, q_ref, k_ref, vt_ref, *refs, n_cast):
    o_ref, lhs_ref = refs[n_cast], refs[-1]
    for src_ref, dst_ref in zip(refs[:n_cast], refs[n_cast + 1:2 * n_cast + 1]):
        dst_ref[...] = src_ref[...].astype(BF16)

    j = pl.program_id(1)
    half_lanes = GROUP * WINDOW
    zeros_top = jnp.zeros((SEL_LANES, half_lanes), F32)
    zeros_bottom = jnp.zeros((HEAD_DIM - SEL_LANES - SLOPE_ROWS, half_lanes), F32)
    for kh in range(N_KV_HEADS):
        feat = jnp.concatenate([zeros_top, slope_ref[kh], zeros_bottom], axis=0).astype(BF16)
        for half in range(2):
            for h in range(GROUP):
                row0 = (kh * GROUP + h) * HEAD_DIM
                lhs_ref[kh, half, 0:HEAD_DIM, h * WINDOW:(h + 1) * WINDOW] = (
                    q_ref[row0:row0 + HEAD_DIM, half * WINDOW:(half + 1) * WINDOW])
            lhs_ref[kh, half, HEAD_DIM:2 * HEAD_DIM, :] = feat

    base = [jnp.maximum(j * MOBA_BLOCK - WINDOW, 0), j * MOBA_BLOCK]
    ones_rows = _ones_rows()
    v_own = vt_ref[j]
    v_prev = vt_ref[jnp.maximum(j - 1, 0)]
    v_straddle = jnp.concatenate([v_prev[:, WINDOW:], v_own[:, :WINDOW]], axis=1)
    v_half = [jnp.where(j > 0, v_straddle, v_own), v_own]

    def keys(kh, half):
        return k_ref[pl.ds(pl.multiple_of(base[half], WINDOW), MOBA_BLOCK), kh * PAIR_W:(kh + 1) * PAIR_W]

    def values(kh, half):
        return jnp.concatenate([v_half[half][kh * HEAD_DIM:(kh + 1) * HEAD_DIM, :], ones_rows], axis=0)

    key = lax.broadcasted_iota(jnp.int32, (MOBA_BLOCK, 2 * WINDOW), 0)
    qry = lax.broadcasted_iota(jnp.int32, (MOBA_BLOCK, 2 * WINDOW), 1) & (WINDOW - 1)
    keep = []
    for half in range(2):
        dist = (j * MOBA_BLOCK + half * WINDOW - base[half]) + qry - key
        keep.append((dist & -WINDOW) == 0)

    lane_q = lax.broadcasted_iota(jnp.int32, (1, half_lanes), 1) & (WINDOW - 1)
    sinks = {}
    for kh in range(N_KV_HEADS):
        slope = slope_ref[kh, 0:1, :] + slope_ref[kh, 1:2, :] + slope_ref[kh, 2:3, :]
        for half in range(2):
            pos = (j * MOBA_BLOCK + half * WINDOW + lane_q).astype(F32)
            sinks[(kh, half)] = sink_ref[kh] + slope * pos

    items = [(kh, half, c) for kh in range(N_KV_HEADS) for c in range(PAIRS) for half in range(2)]
    raw, pvs = {}, {}

    def pair_cols(c):
        return slice(c * 2 * WINDOW, (c + 1) * 2 * WINDOW)

    def scores(i):
        kh, half, c = items[i]
        raw[i] = jnp.dot(keys(kh, half), lhs_ref[kh, half, :, pair_cols(c)],
                         preferred_element_type=F32)

    def softmax_pv(i):
        kh, half, c = items[i]
        s = jnp.where(keep[half], raw.pop(i), NEG_INF)
        sink = sinks[(kh, half)][:, pair_cols(c)]
        m = jnp.maximum(jnp.max(s, axis=0, keepdims=True), sink)
        e = jnp.exp2(s - m).astype(BF16)
        pv = jnp.dot(values(kh, half), e, preferred_element_type=F32)
        pvs[i] = (pv, jnp.exp2(sink - m))

    def normalise(i):
        pv, sink_term = pvs.pop(i)
        l = pv[HEAD_DIM:HEAD_DIM + 1, :] + sink_term
        return pv[0:HEAD_DIM, :] * (1.0 / l)

    done = {}
    for i in range(len(items) + SWA_PV_LAG + 1):
        if i < len(items):
            scores(i)
        if 0 <= i - SWA_PV_LAG < len(items):
            softmax_pv(i - SWA_PV_LAG)
        n = i - SWA_PV_LAG - 1
        if 0 <= n < len(items):
            kh, half, c = items[n]
            done[half] = normalise(n)
            if half == 1:
                pair = jnp.concatenate(
                    [jnp.concatenate([done[0][:, :WINDOW], done[1][:, :WINDOW]], axis=1),
                     jnp.concatenate([done[0][:, WINDOW:], done[1][:, WINDOW:]], axis=1)],
                    axis=0)
                col0 = (kh * PAIRS + c) * PAIR_W
                o_ref[:, col0:col0 + PAIR_W] = pair.T.astype(o_ref.dtype)


def swa_attention(q_t, k_aug, v_t, slope_rows, sink_rows, to_bf16, *, batch, seq):
    n = q_t.shape[1]
    nblk = seq // MOBA_BLOCK
    half_lanes = GROUP * WINDOW
    steps = batch * nblk
    cast_specs = [pl.BlockSpec((w.shape[0], w.shape[1] // steps, w.shape[2]),
                               lambda b, j: (0, b * nblk + j, 0)) for w in to_bf16]
    outs = pl.pallas_call(
        functools.partial(_swa_kernel, n_cast=len(to_bf16)),
        grid=(batch, nblk),
        in_specs=[pl.BlockSpec(slope_rows.shape, lambda b, j: (0, 0, 0)),
                  pl.BlockSpec(sink_rows.shape, lambda b, j: (0, 0, 0)),
                  pl.BlockSpec((Q_DIM, MOBA_BLOCK), lambda b, j: (0, b * nblk + j)),
                  pl.BlockSpec((seq, K_AUG), lambda b, j: (b, 0)),
                  pl.BlockSpec((nblk, KV_DIM, MOBA_BLOCK), lambda b, j: (b, 0, 0))] + cast_specs,
        out_specs=[pl.BlockSpec((MOBA_BLOCK, Q_DIM), lambda b, j: (b * nblk + j, 0))] + cast_specs,
        out_shape=[jax.ShapeDtypeStruct((n, Q_DIM), BF16)]
                  + [jax.ShapeDtypeStruct(w.shape, BF16) for w in to_bf16],
        scratch_shapes=[pltpu.VMEM((N_KV_HEADS, 2, 2 * HEAD_DIM, half_lanes), BF16)],
        compiler_params=_cparams(("parallel", "arbitrary")),
        name="swa_attention",
    )(slope_rows, sink_rows, q_t, k_aug, v_t, *to_bf16)
    return outs[0], outs[1:]


MOBA_PV_LAG = 2


def _moba_kernel(slope_ref, q_ref, k_ref, vt_ref, km_ref, o_ref,
                 lhs_ref, acc_ref, m_ref, s0_ref, s1_ref):
    j = pl.program_id(2)
    lanes = GROUP * MOBA_BLOCK
    nblk = km_ref.shape[1]
    qt = q_ref[...]

    km = km_ref[0][:, :HEAD_DIM]
    gate = jnp.concatenate(
        [jnp.dot(km, qt[h * HEAD_DIM:(h + 1) * HEAD_DIM, :], preferred_element_type=F32)
         for h in range(GROUP)], axis=1)
    blk = lax.broadcasted_iota(jnp.int32, (nblk, lanes), 0)
    blkf = blk.astype(F32)
    g = jnp.where(blk < j, gate, NEG_INF)
    sel = jnp.zeros((nblk, lanes), F32)
    for _ in range(MOBA_TOPK):
        mx = jnp.max(g, axis=0, keepdims=True)
        idx = jnp.min(jnp.where(g == mx, blkf, float(nblk)), axis=0, keepdims=True)
        hit = blkf == idx
        sel = jnp.where(hit, 1.0, sel)
        g = jnp.where(hit, BELOW_NEG_INF, g)
    not_sel = jnp.where(blk < j, 1.0 - sel, jnp.where(blk == j, 0.0, 1.0))
    pad = jnp.zeros((HEAD_DIM - SEL_LANES - SLOPE_ROWS, lanes), F32)
    feat = jnp.concatenate([not_sel, slope_ref[0], pad], axis=0).astype(BF16)
    for h in range(GROUP):
        lhs_ref[0:HEAD_DIM, h * MOBA_BLOCK:(h + 1) * MOBA_BLOCK] = qt[h * HEAD_DIM:(h + 1) * HEAD_DIM, :]
    lhs_ref[HEAD_DIM:2 * HEAD_DIM, :] = feat

    ones_rows = _ones_rows()

    def head_cols(h):
        return slice(h * MOBA_BLOCK, (h + 1) * MOBA_BLOCK)

    def key_block(n):
        return k_ref[pl.ds(pl.multiple_of(n * MOBA_BLOCK, MOBA_BLOCK), MOBA_BLOCK), :]

    def scores(kblk, h):
        return jnp.dot(kblk, lhs_ref[:, head_cols(h)], preferred_element_type=F32)

    def softmax(src_ref, h):
        cols = head_cols(h)
        s = src_ref[:, cols]
        m_old = m_ref[:, cols]
        m_new = jnp.maximum(m_old, jnp.max(s, axis=0, keepdims=True))
        m_ref[:, cols] = m_new
        return h, jnp.exp2(s - m_new).astype(BF16), jnp.exp2(m_old - m_new)

    def accumulate(v_aug, h, e, alpha):
        cols = head_cols(h)
        pv = jnp.dot(v_aug, e, preferred_element_type=F32)
        acc_ref[:, cols] = acc_ref[:, cols] * alpha + pv

    def consume(src_ref, v_aug, h):
        accumulate(v_aug, *softmax(src_ref, h))

    def values(n):
        return jnp.concatenate([vt_ref[n], ones_rows], axis=0)

    def step(t, src_ref, dst_ref):
        kblk = key_block(t)
        v_aug = values(jnp.where(t == 0, j, t - 1))
        pending = []
        for h in range(GROUP):
            dst_ref[:, head_cols(h)] = scores(kblk, h)
            pending.append(softmax(src_ref, h))
            if len(pending) > MOBA_PV_LAG:
                accumulate(v_aug, *pending.pop(0))
        for ready in pending:
            accumulate(v_aug, *ready)

    m_ref[...] = jnp.full(m_ref.shape, RUNNING_MAX_INIT, F32)
    acc_ref[...] = jnp.zeros(acc_ref.shape, F32)
    key = lax.broadcasted_iota(jnp.int32, (MOBA_BLOCK, MOBA_BLOCK), 0)
    qry = lax.broadcasted_iota(jnp.int32, (MOBA_BLOCK, MOBA_BLOCK), 1)
    k_own = key_block(j)
    for h in range(GROUP):
        s0_ref[:, head_cols(h)] = jnp.where(key <= qry, scores(k_own, h), NEG_INF)

    def two_steps(i, carry):
        step(2 * i, s0_ref, s1_ref)
        step(2 * i + 1, s1_ref, s0_ref)
        return carry

    lax.fori_loop(0, j // 2, two_steps, 0)

    def last(src_ref):
        v_aug = values(jnp.where(j == 0, j, j - 1))
        for h in range(GROUP):
            consume(src_ref, v_aug, h)

    @pl.when(j % 2 == 1)
    def _():
        step(j - 1, s0_ref, s1_ref)
        last(s1_ref)

    @pl.when(j % 2 == 0)
    def _():
        last(s0_ref)

    out = acc_ref[0:HEAD_DIM, :] * (1.0 / acc_ref[HEAD_DIM:HEAD_DIM + 1, :])
    _store_pairs(o_ref, out, MOBA_BLOCK)


def moba_attention(q_t, k_aug, v_t, kmean, slope_rows, *, batch, seq):
    n = q_t.shape[1]
    nblk = seq // MOBA_BLOCK
    lanes = GROUP * MOBA_BLOCK
    return pl.pallas_call(
        _moba_kernel,
        grid=(batch, N_KV_HEADS, nblk),
        in_specs=[pl.BlockSpec((1, SLOPE_ROWS, lanes), lambda b, kh, j: (kh, 0, 0)),
                  pl.BlockSpec((GROUP * HEAD_DIM, MOBA_BLOCK), lambda b, kh, j: (kh, b * nblk + j)),
                  pl.BlockSpec((seq, PAIR_W), lambda b, kh, j: (b, kh)),
                  pl.BlockSpec((nblk, HEAD_DIM, MOBA_BLOCK), lambda b, kh, j: (b, kh, 0)),
                  pl.BlockSpec((1, nblk, PAIR_W), lambda b, kh, j: (b, 0, kh))],
        out_specs=pl.BlockSpec((MOBA_BLOCK, GROUP * HEAD_DIM), lambda b, kh, j: (b * nblk + j, kh)),
        out_shape=jax.ShapeDtypeStruct((n, Q_DIM), BF16),
        scratch_shapes=[pltpu.VMEM((2 * HEAD_DIM, lanes), BF16),
                        pltpu.VMEM((HEAD_DIM + ONES_ROWS, lanes), F32),
                        pltpu.VMEM((1, lanes), F32),
                        pltpu.VMEM((MOBA_BLOCK, lanes), F32),
                        pltpu.VMEM((MOBA_BLOCK, lanes), F32)],
        compiler_params=_cparams(("parallel", "parallel", "arbitrary")),
        name="moba_attention",
    )(slope_rows, q_t, k_aug, v_t, kmean)


def _pad_heads(w):
    d = w.shape[0]
    w4 = w.reshape(d, N_KV_HEADS, HEAD_DIM)
    return jnp.concatenate([w4, jnp.zeros_like(w4)], axis=-1).reshape(d, K_AUG)


def _alibi_slopes_base2():
    slopes = np.exp2(-8.0 * np.arange(1, N_HEADS + 1, dtype=np.float32) / N_HEADS)
    return (slopes.astype(np.float32) * np.float32(LOG2E)).astype(np.float32)


def _slope_rows(slopes, width):
    hi = slopes.astype(BF16).astype(np.float32)
    mid = (slopes - hi).astype(BF16).astype(np.float32)
    lo = (slopes - hi - mid).astype(BF16).astype(np.float32)
    pieces = np.stack([hi, mid, lo], axis=-1)
    pad = np.zeros((N_HEADS, SLOPE_ROWS - 2 * SLOPE_PIECES), np.float32)
    rows = np.concatenate([pieces, pieces, pad], axis=1)
    rows = rows.reshape(N_KV_HEADS, GROUP, SLOPE_ROWS).transpose(0, 2, 1)
    return np.repeat(rows, width, axis=2)


def _key_features(seq):
    pos = np.arange(seq, dtype=np.int32)[:, None]
    blk = pos // MOBA_BLOCK
    lane = np.arange(PAIR_W, dtype=np.int32)[None, :] - HEAD_DIM
    feat = np.where(lane == blk, np.float32(NEG_INF), np.float32(0.0))
    feat = np.where((lane >= SLOPE_BLK_LANE) & (lane < SLOPE_POS_LANE),
                    (blk * MOBA_BLOCK).astype(np.float32), feat)
    feat = np.where((lane >= SLOPE_POS_LANE) & (lane < SLOPE_POS_LANE + SLOPE_PIECES),
                    (pos % MOBA_BLOCK).astype(np.float32), feat)
    return np.where(lane < 0, np.float32(0.0), feat).astype(np.float32)


def kernel(x, w_qkv_a, sinks_a, w_o_a, kv_norm, w_kv_shared, w_q_b, w_o_b,
           norm_attn_pre, norm_attn_post, norm_mlp_pre, norm_mlp_post, w_up, w_down):
    batch, seq, d = x.shape
    n = batch * seq
    assert seq // MOBA_BLOCK == SEL_LANES and seq % MOBA_BLOCK == 0
    qscale = HEAD_DIM ** -0.5 * LOG2E
    slopes2 = _alibi_slopes_base2()
    key_feat = _key_features(seq)

    wq_a, wk_a, wv_a = jnp.split(w_qkv_a[0], [Q_DIM, Q_DIM + KV_DIM], axis=-1)
    wk_s, wv_s = jnp.split(w_kv_shared, [KV_DIM], axis=-1)

    xf = x.reshape(n, d)

    q0_t, k0, v0_t = attn_proj(
        xf, norm_attn_pre[0], norm_attn_pre[0], (wq_a * qscale).T.astype(BF16),
        _pad_heads(wk_a).astype(BF16), wv_a.T.astype(BF16), key_feat,
        tm=512, seq=seq, shared_norm=True, with_kmean=False, name="attn_proj0")
    sink_rows = jnp.repeat((sinks_a[0].astype(F32) * LOG2E).reshape(N_KV_HEADS, 1, GROUP), WINDOW, axis=2)
    mix0, (wup_all, wdn_all, wo_a, wo_b) = swa_attention(
        q0_t, k0, v0_t, _slope_rows(slopes2, WINDOW), sink_rows, [w_up, w_down, w_o_a, w_o_b],
        batch=batch, seq=seq)
    x1 = oproj_norm_residual(mix0, xf, norm_attn_post[0], wo_a[0], tm=512, name="oproj0")
    x2 = mlp_block(x1, norm_mlp_pre[0], norm_mlp_post[0], wup_all, wdn_all, 0, tm=512, tf=1024, name="mlp0")

    q1_t, k1, v1_t, kmean = attn_proj(
        x2, norm_attn_pre[1], kv_norm, (w_q_b[0] * qscale).T.astype(BF16),
        _pad_heads(wk_s).astype(BF16), wv_s.T.astype(BF16), key_feat,
        tm=512, seq=seq, shared_norm=False, with_kmean=True, name="attn_proj1")
    kmean = kmean.reshape(batch, seq // MOBA_BLOCK, K_AUG).astype(BF16)
    mix1 = moba_attention(q1_t, k1, v1_t, kmean, _slope_rows(slopes2, MOBA_BLOCK), batch=batch, seq=seq)
    x3 = oproj_norm_residual(mix1, x2, norm_attn_post[1], wo_b[0], tm=512, name="oproj1")
    x4 = mlp_block(x3, norm_mlp_pre[1], norm_mlp_post[1], wup_all, wdn_all, 1, tm=512, tf=1024, name="mlp1")
    return x4.reshape(batch, seq, d)
```

```python
import functools

import jax
import jax.numpy as jnp
import numpy as np
from jax import lax
from jax.experimental import pallas as pl
from jax.experimental.pallas import tpu as pltpu

D_MODEL = 2048
N_HEADS = 32
N_KV_HEADS = 4
HEAD_DIM = 64
GROUP = N_HEADS // N_KV_HEADS
PAIRS = GROUP // 2
PAIR_W = 2 * HEAD_DIM
Q_DIM = N_HEADS * HEAD_DIM
KV_DIM = N_KV_HEADS * HEAD_DIM
K_AUG = N_KV_HEADS * PAIR_W
D_FF = 4 * D_MODEL
WINDOW = 128
MOBA_BLOCK = 256
MOBA_TOPK = 3
RMS_EPS = 1e-6
NEG_INF = -1e30
BELOW_NEG_INF = -3e38
RUNNING_MAX_INIT = -1e38
LOG2E = 1.4426950408889634

SEL_LANES = 16
SLOPE_BLK_LANE = 16
SLOPE_POS_LANE = 19
SLOPE_PIECES = 3
SLOPE_ROWS = 8
ONES_ROWS = 16

VMEM_LIMIT = 56 * 1024 * 1024

BF16 = jnp.bfloat16
F32 = jnp.float32
NT_DIMS = (((1,), (1,)), ((), ()))


def _rms_scale(x):
    return lax.rsqrt(jnp.mean(x * x, axis=-1, keepdims=True) + RMS_EPS)


def _cparams(semantics):
    return pltpu.CompilerParams(dimension_semantics=semantics,
                                vmem_limit_bytes=VMEM_LIMIT)


def _ones_rows():
    row = lax.broadcasted_iota(jnp.int32, (ONES_ROWS, MOBA_BLOCK), 0)
    return jnp.where(row == 0, 1.0, 0.0).astype(BF16)


PROJ_CHUNK = 512


def _attn_proj_kernel(x_ref, gq_ref, gkv_ref, wqt_ref, wk_ref, wvt_ref, feat_ref,
                      qt_ref, k_ref, vt_ref, *km_refs, shared_norm):
    x = x_ref[...]
    xn = x * _rms_scale(x)
    hq = (xn * gq_ref[...]).astype(BF16)
    hkv = hq if shared_norm else (xn * gkv_ref[...]).astype(BF16)
    nblk = x_ref.shape[0] // MOBA_BLOCK
    for c in range(0, wqt_ref.shape[0], PROJ_CHUNK):
        qt = lax.dot_general(wqt_ref[c:c + PROJ_CHUNK, :], hq, NT_DIMS,
                             preferred_element_type=F32).astype(BF16)
        for blk in range(nblk):
            qt_ref[blk, c:c + PROJ_CHUNK, :] = qt[:, blk * MOBA_BLOCK:(blk + 1) * MOBA_BLOCK]
    kacc = jnp.dot(hkv, wk_ref[...], preferred_element_type=F32)
    for km_ref in km_refs:
        for blk in range(nblk):
            rows = kacc[blk * MOBA_BLOCK:(blk + 1) * MOBA_BLOCK]
            km_ref[blk] = jnp.sum(rows, axis=0, keepdims=True) * (1.0 / MOBA_BLOCK)
    feat = feat_ref[...]
    for kh in range(N_KV_HEADS):
        cols = slice(kh * PAIR_W, (kh + 1) * PAIR_W)
        k_ref[:, cols] = (kacc[:, cols] + feat).astype(BF16)
    vt = lax.dot_general(wvt_ref[...], hkv, NT_DIMS, preferred_element_type=F32)
    for blk in range(nblk):
        vt_ref[blk] = vt[:, blk * MOBA_BLOCK:(blk + 1) * MOBA_BLOCK].astype(BF16)


def attn_proj(x, gq, gkv, wq_t, wk_aug, wv_t, key_feat, *, tm, seq, shared_norm, with_kmean, name):
    n, d = x.shape
    nb = tm // MOBA_BLOCK
    tiles_per_seq = seq // tm
    whole = lambda w: pl.BlockSpec(w.shape, lambda i: (0, 0))
    gain = pl.BlockSpec((1, d), lambda i: (0, 0))
    out_specs = [pl.BlockSpec((nb, wq_t.shape[0], MOBA_BLOCK), lambda i: (i, 0, 0)),
                 pl.BlockSpec((tm, K_AUG), lambda i: (i, 0)),
                 pl.BlockSpec((nb, KV_DIM, MOBA_BLOCK), lambda i: (i, 0, 0))]
    out_shape = [jax.ShapeDtypeStruct((n // MOBA_BLOCK, wq_t.shape[0], MOBA_BLOCK), BF16),
                 jax.ShapeDtypeStruct((n, K_AUG), BF16),
                 jax.ShapeDtypeStruct((n // MOBA_BLOCK, KV_DIM, MOBA_BLOCK), BF16)]
    if with_kmean:
        out_specs.append(pl.BlockSpec((nb, 1, K_AUG), lambda i: (i, 0, 0)))
        out_shape.append(jax.ShapeDtypeStruct((n // MOBA_BLOCK, 1, K_AUG), F32))
    return pl.pallas_call(
        functools.partial(_attn_proj_kernel, shared_norm=shared_norm),
        grid=(n // tm,),
        in_specs=[pl.BlockSpec((tm, d), lambda i: (i, 0)), gain, gain,
                  whole(wq_t), whole(wk_aug), whole(wv_t),
                  pl.BlockSpec((tm, PAIR_W), lambda i: (i % tiles_per_seq, 0))],
        out_specs=out_specs,
        out_shape=out_shape,
        compiler_params=_cparams(("parallel",)),
        name=name,
    )(x, gq.reshape(1, d), gkv.reshape(1, d), wq_t, wk_aug, wv_t, key_feat)


def _oproj_kernel(mix_ref, x_ref, g_ref, w_ref, o_ref):
    a = jnp.dot(mix_ref[...], w_ref[...], preferred_element_type=F32)
    o_ref[...] = x_ref[...] + (a * _rms_scale(a)) * g_ref[...]


def oproj_norm_residual(mix, x, g, w, *, tm, name):
    n, d = x.shape
    return pl.pallas_call(
        _oproj_kernel,
        grid=(n // tm,),
        in_specs=[pl.BlockSpec((tm, mix.shape[1]), lambda i: (i, 0)),
                  pl.BlockSpec((tm, d), lambda i: (i, 0)),
                  pl.BlockSpec((1, d), lambda i: (0, 0)),
                  pl.BlockSpec(w.shape, lambda i: (0, 0))],
        out_specs=pl.BlockSpec((tm, d), lambda i: (i, 0)),
        out_shape=jax.ShapeDtypeStruct((n, d), F32),
        compiler_params=_cparams(("parallel",)),
        name=name,
    )(mix, x, g.reshape(1, d), w)


MLP_FF_CHUNK = 512


def _mlp_kernel(x_ref, gpre_ref, gpost_ref, wup_ref, wdn_ref, o_ref, h_ref, acc_ref):
    f = pl.program_id(1)

    @pl.when(f == 0)
    def _():
        x = x_ref[...]
        h_ref[...] = ((x * _rms_scale(x)) * gpre_ref[...]).astype(BF16)
        acc_ref[...] = jnp.zeros(acc_ref.shape, F32)

    h = h_ref[...]
    chunks = wup_ref.shape[1] // MLP_FF_CHUNK

    def up(c):
        cols = slice(c * MLP_FF_CHUNK, (c + 1) * MLP_FF_CHUNK)
        return jnp.dot(h, wup_ref[:, cols], preferred_element_type=F32)

    u_next = up(0)
    for c in range(chunks):
        u = jnp.maximum(u_next, 0.0)
        if c + 1 < chunks:
            u_next = up(c + 1)
        rows = slice(c * MLP_FF_CHUNK, (c + 1) * MLP_FF_CHUNK)
        acc_ref[...] += jnp.dot((u * u).astype(BF16), wdn_ref[rows, :], preferred_element_type=F32)

    @pl.when(f == pl.num_programs(1) - 1)
    def _():
        a = acc_ref[...]
        o_ref[...] = x_ref[...] + (a * _rms_scale(a)) * gpost_ref[...]


def mlp_block(x, gpre, gpost, wup_all, wdn_all, layer, *, tm, tf, name):
    n, d = x.shape
    dff = wup_all.shape[2]
    return pl.pallas_call(
        _mlp_kernel,
        grid=(n // tm, dff // tf),
        in_specs=[pl.BlockSpec((tm, d), lambda i, f: (i, 0)),
                  pl.BlockSpec((1, d), lambda i, f: (0, 0)),
                  pl.BlockSpec((1, d), lambda i, f: (0, 0)),
                  pl.BlockSpec((None, d, tf), lambda i, f: (layer, 0, f)),
                  pl.BlockSpec((None, tf, d), lambda i, f: (layer, f, 0))],
        out_specs=pl.BlockSpec((tm, d), lambda i, f: (i, 0)),
        out_shape=jax.ShapeDtypeStruct((n, d), F32),
        scratch_shapes=[pltpu.VMEM((tm, d), BF16), pltpu.VMEM((tm, d), F32)],
        compiler_params=_cparams(("parallel", "arbitrary")),
        name=name,
    )(x, gpre.reshape(1, d), gpost.reshape(1, d), wup_all, wdn_all)


SWA_PV_LAG = 6


def _swa_kernel(slope_ref, sink_ref, q_ref, k_ref, vt_ref, *refs, n_cast):
    o_ref, lhs_ref = refs[n_cast], refs[-1]
    for src_ref, dst_ref in zip(refs[:n_cast], refs[n_cast + 1:2 * n_cast + 1]):
        dst_ref[...] = src_ref[...].astype(BF16)

    j = pl.program_id(1)
    half_lanes = GROUP * WINDOW
    zeros_top = jnp.zeros((SEL_LANES, half_lanes), F32)
    zeros_bottom = jnp.zeros((HEAD_DIM - SEL_LANES - SLOPE_ROWS, half_lanes), F32)
    for kh in range(N_KV_HEADS):
        feat = jnp.concatenate([zeros_top, slope_ref[kh], zeros_bottom], axis=0).astype(BF16)
        for half in range(2):
            for h in range(GROUP):
                row0 = (kh * GROUP + h) * HEAD_DIM
                lhs_ref[kh, half, 0:HEAD_DIM, h * WINDOW:(h + 1) * WINDOW] = (
                    q_ref[row0:row0 + HEAD_DIM, half * WINDOW:(half + 1) * WINDOW])
            lhs_ref[kh, half, HEAD_DIM:2 * HEAD_DIM, :] = feat

    base = [jnp.maximum(j * MOBA_BLOCK - WINDOW, 0), j * MOBA_BLOCK]
    ones_rows = _ones_rows()
    v_own = vt_ref[j]
    v_prev = vt_ref[jnp.maximum(j - 1, 0)]
    v_straddle = jnp.concatenate([v_prev[:, WINDOW:], v_own[:, :WINDOW]], axis=1)
    v_half = [jnp.where(j > 0, v_straddle, v_own), v_own]

    def keys(kh, half):
        return k_ref[pl.ds(pl.multiple_of(base[half], WINDOW), MOBA_BLOCK), kh * PAIR_W:(kh + 1) * PAIR_W]

    def values(kh, half):
        return jnp.concatenate([v_half[half][kh * HEAD_DIM:(kh + 1) * HEAD_DIM, :], ones_rows], axis=0)

    key = lax.broadcasted_iota(jnp.int32, (MOBA_BLOCK, 2 * WINDOW), 0)
    qry = lax.broadcasted_iota(jnp.int32, (MOBA_BLOCK, 2 * WINDOW), 1) & (WINDOW - 1)
    keep = []
    for half in range(2):
        dist = (j * MOBA_BLOCK + half * WINDOW - base[half]) + qry - key
        keep.append((dist & -WINDOW) == 0)

    lane_q = lax.broadcasted_iota(jnp.int32, (1, half_lanes), 1) & (WINDOW - 1)
    sinks = {}
    for kh in range(N_KV_HEADS):
        slope = slope_ref[kh, 0:1, :] + slope_ref[kh, 1:2, :] + slope_ref[kh, 2:3, :]
        for half in range(2):
            pos = (j * MOBA_BLOCK + half * WINDOW + lane_q).astype(F32)
            sinks[(kh, half)] = sink_ref[kh] + slope * pos

    items = [(kh, half, c) for kh in range(N_KV_HEADS) for c in range(PAIRS) for half in range(2)]
    raw, pvs = {}, {}

    def pair_cols(c):
        return slice(c * 2 * WINDOW, (c + 1) * 2 * WINDOW)

    def scores(i):
        kh, half, c = items[i]
        raw[i] = jnp.dot(keys(kh, half), lhs_ref[kh, half, :, pair_cols(c)],
                         preferred_element_type=F32)

    def softmax_pv(i):
        kh, half, c = items[i]
        s = jnp.where(keep[half], raw.pop(i), NEG_INF)
        sink = sinks[(kh, half)][:, pair_cols(c)]
        m = jnp.maximum(jnp.max(s, axis=0, keepdims=True), sink)
        e = jnp.exp2(s - m).astype(BF16)
        pv = jnp.dot(values(kh, half), e, preferred_element_type=F32)
        pvs[i] = (pv, jnp.exp2(sink - m))

    def normalise(i):
        pv, sink_term = pvs.pop(i)
        l = pv[HEAD_DIM:HEAD_DIM + 1, :] + sink_term
        return pv[0:HEAD_DIM, :] * (1.0 / l)

    done = {}
    for i in range(len(items) + SWA_PV_LAG + 1):
        if i < len(items):
            scores(i)
        if 0 <= i - SWA_PV_LAG < len(items):
            softmax_pv(i - SWA_PV_LAG)
        n = i - SWA_PV_LAG - 1
        if 0 <= n < len(items):
            kh, half, c = items[n]
            done[half] = normalise(n)
            if half == 1:
                pair = jnp.concatenate(
                    [jnp.concatenate([done[0][:, :WINDOW], done[1][:, :WINDOW]], axis=1),
                     jnp.concatenate([done[0][:, WINDOW:], done[1][:, WINDOW:]], axis=1)],
                    axis=0)
                col0 = (kh * PAIRS + c) * PAIR_W
                o_ref[:, col0:col0 + PAIR_W] = pair.T.astype(o_ref.dtype)


def swa_attention(q_t, k_aug, v_t, slope_rows, sink_rows, to_bf16, *, batch, seq):
    n = k_aug.shape[0]
    nblk = seq // MOBA_BLOCK
    half_lanes = GROUP * WINDOW
    steps = batch * nblk
    cast_specs = [pl.BlockSpec((w.shape[0], w.shape[1] // steps, w.shape[2]),
                               lambda b, j: (0, b * nblk + j, 0)) for w in to_bf16]
    outs = pl.pallas_call(
        functools.partial(_swa_kernel, n_cast=len(to_bf16)),
        grid=(batch, nblk),
        in_specs=[pl.BlockSpec(slope_rows.shape, lambda b, j: (0, 0, 0)),
                  pl.BlockSpec(sink_rows.shape, lambda b, j: (0, 0, 0)),
                  pl.BlockSpec((None, Q_DIM, MOBA_BLOCK), lambda b, j: (b * nblk + j, 0, 0)),
                  pl.BlockSpec((seq, K_AUG), lambda b, j: (b, 0)),
                  pl.BlockSpec((nblk, KV_DIM, MOBA_BLOCK), lambda b, j: (b, 0, 0))] + cast_specs,
        out_specs=[pl.BlockSpec((MOBA_BLOCK, Q_DIM), lambda b, j: (b * nblk + j, 0))] + cast_specs,
        out_shape=[jax.ShapeDtypeStruct((n, Q_DIM), BF16)]
                  + [jax.ShapeDtypeStruct(w.shape, BF16) for w in to_bf16],
        scratch_shapes=[pltpu.VMEM((N_KV_HEADS, 2, 2 * HEAD_DIM, half_lanes), BF16)],
        compiler_params=_cparams(("parallel", "arbitrary")),
        name="swa_attention",
    )(slope_rows, sink_rows, q_t, k_aug, v_t, *to_bf16)
    return outs[0], outs[1:]


MOBA_PV_LAG = 2


def _moba_kernel(slope_ref, q_ref, k_ref, vt_ref, km_ref, o_ref,
                 lhs_ref, acc_ref, m_ref, s0_ref, s1_ref):
    lanes = GROUP * MOBA_BLOCK
    nblk = km_ref.shape[1]
    km = km_ref[0][:, :HEAD_DIM]
    blk = lax.broadcasted_iota(jnp.int32, (nblk, lanes), 0)
    blkf = blk.astype(F32)
    pad = jnp.zeros((HEAD_DIM - SEL_LANES - SLOPE_ROWS, lanes), F32)

    def build_operand(j, carry):
        qt = q_ref[j]
        gate = jnp.concatenate(
            [jnp.dot(km, qt[h * HEAD_DIM:(h + 1) * HEAD_DIM, :], preferred_element_type=F32)
             for h in range(GROUP)], axis=1)
        g = jnp.where(blk < j, gate, NEG_INF)
        sel = jnp.zeros((nblk, lanes), F32)
        for _ in range(MOBA_TOPK):
            mx = jnp.max(g, axis=0, keepdims=True)
            idx = jnp.min(jnp.where(g == mx, blkf, float(nblk)), axis=0, keepdims=True)
            hit = blkf == idx
            sel = jnp.where(hit, 1.0, sel)
            g = jnp.where(hit, BELOW_NEG_INF, g)
        not_sel = jnp.where(blk < j, 1.0 - sel, jnp.where(blk == j, 0.0, 1.0))
        feat = jnp.concatenate([not_sel, slope_ref[0], pad], axis=0).astype(BF16)
        for h in range(GROUP):
            lhs_ref[j, h, 0:HEAD_DIM, :] = qt[h * HEAD_DIM:(h + 1) * HEAD_DIM, :]
            lhs_ref[j, h, HEAD_DIM:2 * HEAD_DIM, :] = feat[:, h * MOBA_BLOCK:(h + 1) * MOBA_BLOCK]
        return carry

    lax.fori_loop(0, nblk, build_operand, 0)
    m_ref[...] = jnp.full(m_ref.shape, RUNNING_MAX_INIT, F32)
    acc_ref[...] = jnp.zeros(acc_ref.shape, F32)

    ones_rows = _ones_rows()
    key = lax.broadcasted_iota(jnp.int32, (MOBA_BLOCK, MOBA_BLOCK), 0)
    qry = lax.broadcasted_iota(jnp.int32, (MOBA_BLOCK, MOBA_BLOCK), 1)

    def key_block(n):
        return k_ref[pl.ds(pl.multiple_of(n * MOBA_BLOCK, MOBA_BLOCK), MOBA_BLOCK), :]

    def values(n):
        return jnp.concatenate([vt_ref[n], ones_rows], axis=0)

    def softmax(src_ref, j, h):
        s = src_ref[h]
        m_old = m_ref[j, h]
        m_new = jnp.maximum(m_old, jnp.max(s, axis=0, keepdims=True))
        m_ref[j, h] = m_new
        return j, h, jnp.exp2(s - m_new).astype(BF16), jnp.exp2(m_old - m_new)

    def accumulate(v_aug, j, h, e, alpha):
        pv = jnp.dot(v_aug, e, preferred_element_type=F32)
        acc_ref[j, h] = acc_ref[j, h] * alpha + pv

    def step(j, t, src_ref, dst_ref, next_own):
        if next_own:
            jn = jnp.minimum(j + 1, nblk - 1)
            kblk = key_block(jn)
        else:
            jn = j
            kblk = key_block(t)
        v_aug = values(jnp.where(t == 0, j, t - 1))
        pending = []
        for h in range(GROUP):
            s_next = jnp.dot(kblk, lhs_ref[jn, h], preferred_element_type=F32)
            dst_ref[h] = jnp.where(key <= qry, s_next, NEG_INF) if next_own else s_next
            pending.append(softmax(src_ref, j, h))
            if len(pending) > MOBA_PV_LAG:
                accumulate(v_aug, *pending.pop(0))
        for ready in pending:
            accumulate(v_aug, *ready)

    def query_block(j, cur_ref, other_ref, odd):
        def two_steps(i, carry):
            step(j, 2 * i, cur_ref, other_ref, False)
            step(j, 2 * i + 1, other_ref, cur_ref, False)
            return carry

        lax.fori_loop(0, j // 2, two_steps, 0)
        if odd:
            step(j, j - 1, cur_ref, other_ref, False)
            step(j, j, other_ref, cur_ref, True)
        else:
            step(j, j, cur_ref, other_ref, True)

    k_first = key_block(0)
    for h in range(GROUP):
        s0_ref[h] = jnp.where(key <= qry, jnp.dot(k_first, lhs_ref[0, h], preferred_element_type=F32),
                              NEG_INF)

    def four_query_blocks(i, carry):
        query_block(4 * i, s0_ref, s1_ref, False)
        query_block(4 * i + 1, s1_ref, s0_ref, True)
        query_block(4 * i + 2, s1_ref, s0_ref, False)
        query_block(4 * i + 3, s0_ref, s1_ref, True)
        return carry

    lax.fori_loop(0, nblk // 4, four_query_blocks, 0)

    def write_out(j, carry):
        rows = pl.ds(pl.multiple_of(j * MOBA_BLOCK, MOBA_BLOCK), MOBA_BLOCK)
        for p in range(PAIRS):
            pair = jnp.concatenate(
                [acc_ref[j, h, 0:HEAD_DIM, :] * (1.0 / acc_ref[j, h, HEAD_DIM:HEAD_DIM + 1, :])
                 for h in (2 * p, 2 * p + 1)], axis=0)
            o_ref[rows, p * PAIR_W:(p + 1) * PAIR_W] = pair.T.astype(o_ref.dtype)
        return carry

    lax.fori_loop(0, nblk, write_out, 0)


def moba_attention(q_t, k_aug, v_t, kmean, slope_rows, *, batch, seq):
    n = k_aug.shape[0]
    nblk = seq // MOBA_BLOCK
    assert nblk % 4 == 0
    lanes = GROUP * MOBA_BLOCK
    return pl.pallas_call(
        _moba_kernel,
        grid=(batch, N_KV_HEADS),
        in_specs=[pl.BlockSpec((1, SLOPE_ROWS, lanes), lambda b, kh: (kh, 0, 0)),
                  pl.BlockSpec((nblk, GROUP * HEAD_DIM, MOBA_BLOCK), lambda b, kh: (b, kh, 0)),
                  pl.BlockSpec((seq, PAIR_W), lambda b, kh: (b, kh)),
                  pl.BlockSpec((nblk, HEAD_DIM, MOBA_BLOCK), lambda b, kh: (b, kh, 0)),
                  pl.BlockSpec((1, nblk, PAIR_W), lambda b, kh: (b, 0, kh))],
        out_specs=pl.BlockSpec((seq, GROUP * HEAD_DIM), lambda b, kh: (b, kh)),
        out_shape=jax.ShapeDtypeStruct((n, Q_DIM), BF16),
        scratch_shapes=[pltpu.VMEM((nblk, GROUP, 2 * HEAD_DIM, MOBA_BLOCK), BF16),
                        pltpu.VMEM((nblk, GROUP, HEAD_DIM + ONES_ROWS, MOBA_BLOCK), F32),
                        pltpu.VMEM((nblk, GROUP, 1, MOBA_BLOCK), F32),
                        pltpu.VMEM((GROUP, MOBA_BLOCK, MOBA_BLOCK), F32),
                        pltpu.VMEM((GROUP, MOBA_BLOCK, MOBA_BLOCK), F32)],
        compiler_params=_cparams(("parallel", "arbitrary")),
        name="moba_attention",
    )(slope_rows, q_t, k_aug, v_t, kmean)


def _pad_heads(w):
    d = w.shape[0]
    w4 = w.reshape(d, N_KV_HEADS, HEAD_DIM)
    return jnp.concatenate([w4, jnp.zeros_like(w4)], axis=-1).reshape(d, K_AUG)


def _alibi_slopes_base2():
    slopes = np.exp2(-8.0 * np.arange(1, N_HEADS + 1, dtype=np.float32) / N_HEADS)
    return (slopes.astype(np.float32) * np.float32(LOG2E)).astype(np.float32)


def _slope_rows(slopes, width):
    hi = slopes.astype(BF16).astype(np.float32)
    mid = (slopes - hi).astype(BF16).astype(np.float32)
    lo = (slopes - hi - mid).astype(BF16).astype(np.float32)
    pieces = np.stack([hi, mid, lo], axis=-1)
    pad = np.zeros((N_HEADS, SLOPE_ROWS - 2 * SLOPE_PIECES), np.float32)
    rows = np.concatenate([pieces, pieces, pad], axis=1)
    rows = rows.reshape(N_KV_HEADS, GROUP, SLOPE_ROWS).transpose(0, 2, 1)
    return np.repeat(rows, width, axis=2)


def _key_features(seq):
    pos = np.arange(seq, dtype=np.int32)[:, None]
    blk = pos // MOBA_BLOCK
    lane = np.arange(PAIR_W, dtype=np.int32)[None, :] - HEAD_DIM
    feat = np.where(lane == blk, np.float32(NEG_INF), np.float32(0.0))
    feat = np.where((lane >= SLOPE_BLK_LANE) & (lane < SLOPE_POS_LANE),
                    (blk * MOBA_BLOCK).astype(np.float32), feat)
    feat = np.where((lane >= SLOPE_POS_LANE) & (lane < SLOPE_POS_LANE + SLOPE_PIECES),
                    (pos % MOBA_BLOCK).astype(np.float32), feat)
    return np.where(lane < 0, np.float32(0.0), feat).astype(np.float32)


def kernel(x, w_qkv_a, sinks_a, w_o_a, kv_norm, w_kv_shared, w_q_b, w_o_b,
           norm_attn_pre, norm_attn_post, norm_mlp_pre, norm_mlp_post, w_up, w_down):
    batch, seq, d = x.shape
    n = batch * seq
    assert seq // MOBA_BLOCK == SEL_LANES and seq % MOBA_BLOCK == 0
    qscale = HEAD_DIM ** -0.5 * LOG2E
    slopes2 = _alibi_slopes_base2()
    key_feat = _key_features(seq)

    wq_a, wk_a, wv_a = jnp.split(w_qkv_a[0], [Q_DIM, Q_DIM + KV_DIM], axis=-1)
    wk_s, wv_s = jnp.split(w_kv_shared, [KV_DIM], axis=-1)

    xf = x.reshape(n, d)

    q0_t, k0, v0_t = attn_proj(
        xf, norm_attn_pre[0], norm_attn_pre[0], (wq_a * qscale).T.astype(BF16),
        _pad_heads(wk_a).astype(BF16), wv_a.T.astype(BF16), key_feat,
        tm=512, seq=seq, shared_norm=True, with_kmean=False, name="attn_proj0")
    sink_rows = jnp.repeat((sinks_a[0].astype(F32) * LOG2E).reshape(N_KV_HEADS, 1, GROUP), WINDOW, axis=2)
    mix0, (wup_all, wdn_all, wo_a, wo_b) = swa_attention(
        q0_t, k0, v0_t, _slope_rows(slopes2, WINDOW), sink_rows, [w_up, w_down, w_o_a, w_o_b],
        batch=batch, seq=seq)
    x1 = oproj_norm_residual(mix0, xf, norm_attn_post[0], wo_a[0], tm=512, name="oproj0")
    x2 = mlp_block(x1, norm_mlp_pre[0], norm_mlp_post[0], wup_all, wdn_all, 0, tm=512, tf=1024, name="mlp0")

    q1_t, k1, v1_t, kmean = attn_proj(
        x2, norm_attn_pre[1], kv_norm, (w_q_b[0] * qscale).T.astype(BF16),
        _pad_heads(wk_s).astype(BF16), wv_s.T.astype(BF16), key_feat,
        tm=512, seq=seq, shared_norm=False, with_kmean=True, name="attn_proj1")
    kmean = kmean.reshape(batch, seq // MOBA_BLOCK, K_AUG).astype(BF16)
    mix1 = moba_attention(q1_t, k1, v1_t, kmean, _slope_rows(slopes2, MOBA_BLOCK), batch=batch, seq=seq)
    x3 = oproj_norm_residual(mix1, x2, norm_attn_post[1], wo_b[0], tm=512, name="oproj1")
    x4 = mlp_block(x3, norm_mlp_pre[1], norm_mlp_post[1], wup_all, wdn_all, 1, tm=512, tf=1024, name="mlp1")
    return x4.reshape(batch, seq, d)
```

```python
import functools

import jax
import jax.numpy as jnp
import numpy as np
from jax import lax
from jax.experimental import pallas as pl
from jax.experimental.pallas import tpu as pltpu

D_MODEL = 2048
N_HEADS = 32
N_KV_HEADS = 4
HEAD_DIM = 64
GROUP = N_HEADS // N_KV_HEADS
PAIRS = GROUP // 2
PAIR_W = 2 * HEAD_DIM
Q_DIM = N_HEADS * HEAD_DIM
KV_DIM = N_KV_HEADS * HEAD_DIM
K_AUG = N_KV_HEADS * PAIR_W
D_FF = 4 * D_MODEL
WINDOW = 128
MOBA_BLOCK = 256
MOBA_TOPK = 3
RMS_EPS = 1e-6
NEG_INF = -1e30
BELOW_NEG_INF = -3e38
RUNNING_MAX_INIT = -1e38
LOG2E = 1.4426950408889634

SEL_LANES = 16
SLOPE_BLK_LANE = 16
SLOPE_POS_LANE = 19
SLOPE_PIECES = 3
SLOPE_ROWS = 8
ONES_ROWS = 16

VMEM_LIMIT = 56 * 1024 * 1024

BF16 = jnp.bfloat16
F32 = jnp.float32
NT_DIMS = (((1,), (1,)), ((), ()))


def _rms_scale(x):
    return lax.rsqrt(jnp.mean(x * x, axis=-1, keepdims=True) + RMS_EPS)


def _cparams(semantics):
    return pltpu.CompilerParams(dimension_semantics=semantics,
                                vmem_limit_bytes=VMEM_LIMIT)


def _ones_rows():
    row = lax.broadcasted_iota(jnp.int32, (ONES_ROWS, MOBA_BLOCK), 0)
    return jnp.where(row == 0, 1.0, 0.0).astype(BF16)


PROJ_CHUNK = 512


def _attn_proj_kernel(x_ref, gq_ref, gkv_ref, wqt_ref, wk_ref, wvt_ref, feat_ref,
                      qt_ref, k_ref, vt_ref, *km_refs, shared_norm):
    x = x_ref[...]
    xn = x * _rms_scale(x)
    hq = (xn * gq_ref[...]).astype(BF16)
    hkv = hq if shared_norm else (xn * gkv_ref[...]).astype(BF16)
    nblk = x_ref.shape[0] // MOBA_BLOCK
    for c in range(0, wqt_ref.shape[0], PROJ_CHUNK):
        qt = lax.dot_general(wqt_ref[c:c + PROJ_CHUNK, :], hq, NT_DIMS,
                             preferred_element_type=F32).astype(BF16)
        for blk in range(nblk):
            qt_ref[blk, c:c + PROJ_CHUNK, :] = qt[:, blk * MOBA_BLOCK:(blk + 1) * MOBA_BLOCK]
    kacc = jnp.dot(hkv, wk_ref[...], preferred_element_type=F32)
    feat = feat_ref[...]
    lane = lax.broadcasted_iota(jnp.int32, feat.shape, 1)
    for kh in range(N_KV_HEADS):
        two_heads = kacc[:, (kh // 2) * PAIR_W:(kh // 2 + 1) * PAIR_W]
        if kh % 2:
            two_heads = jnp.concatenate([two_heads[:, HEAD_DIM:], two_heads[:, :HEAD_DIM]], axis=1)
        k_head = jnp.where(lane < HEAD_DIM, two_heads, 0.0)
        cols = slice(kh * PAIR_W, (kh + 1) * PAIR_W)
        for km_ref in km_refs:
            for blk in range(nblk):
                rows = k_head[blk * MOBA_BLOCK:(blk + 1) * MOBA_BLOCK]
                km_ref[blk, :, cols] = jnp.sum(rows, axis=0, keepdims=True) * (1.0 / MOBA_BLOCK)
        k_ref[:, cols] = (k_head + feat).astype(BF16)
    vt = lax.dot_general(wvt_ref[...], hkv, NT_DIMS, preferred_element_type=F32)
    for blk in range(nblk):
        vt_ref[blk] = vt[:, blk * MOBA_BLOCK:(blk + 1) * MOBA_BLOCK].astype(BF16)


def attn_proj(x, gq, gkv, wq_t, wk, wv_t, key_feat, *, tm, seq, shared_norm, with_kmean, name):
    n, d = x.shape
    nb = tm // MOBA_BLOCK
    tiles_per_seq = seq // tm
    whole = lambda w: pl.BlockSpec(w.shape, lambda i: (0, 0))
    gain = pl.BlockSpec((1, d), lambda i: (0, 0))
    out_specs = [pl.BlockSpec((nb, wq_t.shape[0], MOBA_BLOCK), lambda i: (i, 0, 0)),
                 pl.BlockSpec((tm, K_AUG), lambda i: (i, 0)),
                 pl.BlockSpec((nb, KV_DIM, MOBA_BLOCK), lambda i: (i, 0, 0))]
    out_shape = [jax.ShapeDtypeStruct((n // MOBA_BLOCK, wq_t.shape[0], MOBA_BLOCK), BF16),
                 jax.ShapeDtypeStruct((n, K_AUG), BF16),
                 jax.ShapeDtypeStruct((n // MOBA_BLOCK, KV_DIM, MOBA_BLOCK), BF16)]
    if with_kmean:
        out_specs.append(pl.BlockSpec((nb, 1, K_AUG), lambda i: (i, 0, 0)))
        out_shape.append(jax.ShapeDtypeStruct((n // MOBA_BLOCK, 1, K_AUG), F32))
    return pl.pallas_call(
        functools.partial(_attn_proj_kernel, shared_norm=shared_norm),
        grid=(n // tm,),
        in_specs=[pl.BlockSpec((tm, d), lambda i: (i, 0)), gain, gain,
                  whole(wq_t), whole(wk), whole(wv_t),
                  pl.BlockSpec((tm, PAIR_W), lambda i: (i % tiles_per_seq, 0))],
        out_specs=out_specs,
        out_shape=out_shape,
        compiler_params=_cparams(("parallel",)),
        name=name,
    )(x, gq.reshape(1, d), gkv.reshape(1, d), wq_t, wk, wv_t, key_feat)


def _oproj_kernel(mix_ref, x_ref, g_ref, w_ref, o_ref):
    a = jnp.dot(mix_ref[...], w_ref[...], preferred_element_type=F32)
    o_ref[...] = x_ref[...] + (a * _rms_scale(a)) * g_ref[...]


def oproj_norm_residual(mix, x, g, w, *, tm, name):
    n, d = x.shape
    return pl.pallas_call(
        _oproj_kernel,
        grid=(n // tm,),
        in_specs=[pl.BlockSpec((tm, mix.shape[1]), lambda i: (i, 0)),
                  pl.BlockSpec((tm, d), lambda i: (i, 0)),
                  pl.BlockSpec((1, d), lambda i: (0, 0)),
                  pl.BlockSpec(w.shape, lambda i: (0, 0))],
        out_specs=pl.BlockSpec((tm, d), lambda i: (i, 0)),
        out_shape=jax.ShapeDtypeStruct((n, d), F32),
        compiler_params=_cparams(("parallel",)),
        name=name,
    )(mix, x, g.reshape(1, d), w)


MLP_FF_CHUNK = 512


def _mlp_kernel(x_ref, gpre_ref, gpost_ref, wup_ref, wdn_ref, o_ref, h_ref, acc_ref):
    f = pl.program_id(1)

    @pl.when(f == 0)
    def _():
        x = x_ref[...]
        h_ref[...] = ((x * _rms_scale(x)) * gpre_ref[...]).astype(BF16)
        acc_ref[...] = jnp.zeros(acc_ref.shape, F32)

    h = h_ref[...]
    chunks = wup_ref.shape[1] // MLP_FF_CHUNK

    def up(c):
        cols = slice(c * MLP_FF_CHUNK, (c + 1) * MLP_FF_CHUNK)
        return jnp.dot(h, wup_ref[:, cols], preferred_element_type=F32)

    u_next = up(0)
    for c in range(chunks):
        u = jnp.maximum(u_next, 0.0)
        if c + 1 < chunks:
            u_next = up(c + 1)
        rows = slice(c * MLP_FF_CHUNK, (c + 1) * MLP_FF_CHUNK)
        acc_ref[...] += jnp.dot((u * u).astype(BF16), wdn_ref[rows, :], preferred_element_type=F32)

    @pl.when(f == pl.num_programs(1) - 1)
    def _():
        a = acc_ref[...]
        o_ref[...] = x_ref[...] + (a * _rms_scale(a)) * gpost_ref[...]


def mlp_block(x, gpre, gpost, wup_all, wdn_all, layer, *, tm, tf, name):
    n, d = x.shape
    dff = wup_all.shape[2]
    return pl.pallas_call(
        _mlp_kernel,
        grid=(n // tm, dff // tf),
        in_specs=[pl.BlockSpec((tm, d), lambda i, f: (i, 0)),
                  pl.BlockSpec((1, d), lambda i, f: (0, 0)),
                  pl.BlockSpec((1, d), lambda i, f: (0, 0)),
                  pl.BlockSpec((None, d, tf), lambda i, f: (layer, 0, f)),
                  pl.BlockSpec((None, tf, d), lambda i, f: (layer, f, 0))],
        out_specs=pl.BlockSpec((tm, d), lambda i, f: (i, 0)),
        out_shape=jax.ShapeDtypeStruct((n, d), F32),
        scratch_shapes=[pltpu.VMEM((tm, d), BF16), pltpu.VMEM((tm, d), F32)],
        compiler_params=_cparams(("parallel", "arbitrary")),
        name=name,
    )(x, gpre.reshape(1, d), gpost.reshape(1, d), wup_all, wdn_all)


SWA_PV_LAG = 6


def _swa_kernel(slope_ref, sink_ref, q_ref, k_ref, vt_ref, *refs, n_cast):
    o_ref, lhs_ref = refs[n_cast], refs[-1]
    for src_ref, dst_ref in zip(refs[:n_cast], refs[n_cast + 1:2 * n_cast + 1]):
        dst_ref[...] = src_ref[...].astype(BF16)

    j = pl.program_id(1)
    half_lanes = GROUP * WINDOW
    zeros_top = jnp.zeros((SEL_LANES, half_lanes), F32)
    zeros_bottom = jnp.zeros((HEAD_DIM - SEL_LANES - SLOPE_ROWS, half_lanes), F32)
    for kh in range(N_KV_HEADS):
        feat = jnp.concatenate([zeros_top, slope_ref[kh], zeros_bottom], axis=0).astype(BF16)
        for half in range(2):
            for h in range(GROUP):
                row0 = (kh * GROUP + h) * HEAD_DIM
                lhs_ref[kh, half, 0:HEAD_DIM, h * WINDOW:(h + 1) * WINDOW] = (
                    q_ref[row0:row0 + HEAD_DIM, half * WINDOW:(half + 1) * WINDOW])
            lhs_ref[kh, half, HEAD_DIM:2 * HEAD_DIM, :] = feat

    base = [jnp.maximum(j * MOBA_BLOCK - WINDOW, 0), j * MOBA_BLOCK]
    ones_rows = _ones_rows()
    v_own = vt_ref[j]
    v_prev = vt_ref[jnp.maximum(j - 1, 0)]
    v_straddle = jnp.concatenate([v_prev[:, WINDOW:], v_own[:, :WINDOW]], axis=1)
    v_half = [jnp.where(j > 0, v_straddle, v_own), v_own]

    def keys(kh, half):
        return k_ref[pl.ds(pl.multiple_of(base[half], WINDOW), MOBA_BLOCK), kh * PAIR_W:(kh + 1) * PAIR_W]

    def values(kh, half):
        return jnp.concatenate([v_half[half][kh * HEAD_DIM:(kh + 1) * HEAD_DIM, :], ones_rows], axis=0)

    key = lax.broadcasted_iota(jnp.int32, (MOBA_BLOCK, 2 * WINDOW), 0)
    qry = lax.broadcasted_iota(jnp.int32, (MOBA_BLOCK, 2 * WINDOW), 1) & (WINDOW - 1)
    keep = []
    for half in range(2):
        dist = (j * MOBA_BLOCK + half * WINDOW - base[half]) + qry - key
        keep.append((dist & -WINDOW) == 0)

    lane_q = lax.broadcasted_iota(jnp.int32, (1, half_lanes), 1) & (WINDOW - 1)
    sinks = {}
    for kh in range(N_KV_HEADS):
        slope = slope_ref[kh, 0:1, :] + slope_ref[kh, 1:2, :] + slope_ref[kh, 2:3, :]
        for half in range(2):
            pos = (j * MOBA_BLOCK + half * WINDOW + lane_q).astype(F32)
            sinks[(kh, half)] = sink_ref[kh] + slope * pos

    items = [(kh, half, c) for kh in range(N_KV_HEADS) for c in range(PAIRS) for half in range(2)]
    raw, pvs = {}, {}

    def pair_cols(c):
        return slice(c * 2 * WINDOW, (c + 1) * 2 * WINDOW)

    def scores(i):
        kh, half, c = items[i]
        raw[i] = jnp.dot(keys(kh, half), lhs_ref[kh, half, :, pair_cols(c)],
                         preferred_element_type=F32)

    def softmax_pv(i):
        kh, half, c = items[i]
        s = jnp.where(keep[half], raw.pop(i), NEG_INF)
        sink = sinks[(kh, half)][:, pair_cols(c)]
        m = jnp.maximum(jnp.max(s, axis=0, keepdims=True), sink)
        e = jnp.exp2(s - m).astype(BF16)
        pv = jnp.dot(values(kh, half), e, preferred_element_type=F32)
        pvs[i] = (pv, jnp.exp2(sink - m))

    def normalise(i):
        pv, sink_term = pvs.pop(i)
        l = pv[HEAD_DIM:HEAD_DIM + 1, :] + sink_term
        return pv[0:HEAD_DIM, :] * (1.0 / l)

    done = {}
    for i in range(len(items) + SWA_PV_LAG + 1):
        if i < len(items):
            scores(i)
        if 0 <= i - SWA_PV_LAG < len(items):
            softmax_pv(i - SWA_PV_LAG)
        n = i - SWA_PV_LAG - 1
        if 0 <= n < len(items):
            kh, half, c = items[n]
            done[half] = normalise(n)
            if half == 1:
                pair = jnp.concatenate(
                    [jnp.concatenate([done[0][:, :WINDOW], done[1][:, :WINDOW]], axis=1),
                     jnp.concatenate([done[0][:, WINDOW:], done[1][:, WINDOW:]], axis=1)],
                    axis=0)
                col0 = (kh * PAIRS + c) * PAIR_W
                o_ref[:, col0:col0 + PAIR_W] = pair.T.astype(o_ref.dtype)


def swa_attention(q_t, k_aug, v_t, slope_rows, sink_rows, to_bf16, *, batch, seq):
    n = k_aug.shape[0]
    nblk = seq // MOBA_BLOCK
    half_lanes = GROUP * WINDOW
    steps = batch * nblk
    cast_specs = [pl.BlockSpec((w.shape[0], w.shape[1] // steps, w.shape[2]),
                               lambda b, j: (0, b * nblk + j, 0)) for w in to_bf16]
    outs = pl.pallas_call(
        functools.partial(_swa_kernel, n_cast=len(to_bf16)),
        grid=(batch, nblk),
        in_specs=[pl.BlockSpec(slope_rows.shape, lambda b, j: (0, 0, 0)),
                  pl.BlockSpec(sink_rows.shape, lambda b, j: (0, 0, 0)),
                  pl.BlockSpec((None, Q_DIM, MOBA_BLOCK), lambda b, j: (b * nblk + j, 0, 0)),
                  pl.BlockSpec((seq, K_AUG), lambda b, j: (b, 0)),
                  pl.BlockSpec((nblk, KV_DIM, MOBA_BLOCK), lambda b, j: (b, 0, 0))] + cast_specs,
        out_specs=[pl.BlockSpec((MOBA_BLOCK, Q_DIM), lambda b, j: (b * nblk + j, 0))] + cast_specs,
        out_shape=[jax.ShapeDtypeStruct((n, Q_DIM), BF16)]
                  + [jax.ShapeDtypeStruct(w.shape, BF16) for w in to_bf16],
        scratch_shapes=[pltpu.VMEM((N_KV_HEADS, 2, 2 * HEAD_DIM, half_lanes), BF16)],
        compiler_params=_cparams(("parallel", "arbitrary")),
        name="swa_attention",
    )(slope_rows, sink_rows, q_t, k_aug, v_t, *to_bf16)
    return outs[0], outs[1:]


MOBA_PV_LAG = 2


def _moba_kernel(slope_ref, q_ref, k_ref, vt_ref, km_ref, o_ref,
                 lhs_ref, acc_ref, m_ref, s0_ref, s1_ref):
    lanes = GROUP * MOBA_BLOCK
    nblk = km_ref.shape[1]
    km = km_ref[0][:, :HEAD_DIM]
    blk = lax.broadcasted_iota(jnp.int32, (nblk, lanes), 0)
    blkf = blk.astype(F32)
    pad = jnp.zeros((HEAD_DIM - SEL_LANES - SLOPE_ROWS, lanes), F32)

    def build_operand(j, carry):
        qt = q_ref[j]
        gate = jnp.concatenate(
            [jnp.dot(km, qt[h * HEAD_DIM:(h + 1) * HEAD_DIM, :], preferred_element_type=F32)
             for h in range(GROUP)], axis=1)
        g = jnp.where(blk < j, gate, NEG_INF)
        sel = jnp.zeros((nblk, lanes), F32)
        for _ in range(MOBA_TOPK):
            mx = jnp.max(g, axis=0, keepdims=True)
            idx = jnp.min(jnp.where(g == mx, blkf, float(nblk)), axis=0, keepdims=True)
            hit = blkf == idx
            sel = jnp.where(hit, 1.0, sel)
            g = jnp.where(hit, BELOW_NEG_INF, g)
        not_sel = jnp.where(blk < j, 1.0 - sel, jnp.where(blk == j, 0.0, 1.0))
        feat = jnp.concatenate([not_sel, slope_ref[0], pad], axis=0).astype(BF16)
        for h in range(GROUP):
            lhs_ref[j, h, 0:HEAD_DIM, :] = qt[h * HEAD_DIM:(h + 1) * HEAD_DIM, :]
            lhs_ref[j, h, HEAD_DIM:2 * HEAD_DIM, :] = feat[:, h * MOBA_BLOCK:(h + 1) * MOBA_BLOCK]
        return carry

    lax.fori_loop(0, nblk, build_operand, 0, unroll=4)
    m_ref[...] = jnp.full(m_ref.shape, RUNNING_MAX_INIT, F32)
    acc_ref[...] = jnp.zeros(acc_ref.shape, F32)

    ones_rows = _ones_rows()
    key = lax.broadcasted_iota(jnp.int32, (MOBA_BLOCK, MOBA_BLOCK), 0)
    qry = lax.broadcasted_iota(jnp.int32, (MOBA_BLOCK, MOBA_BLOCK), 1)

    def key_block(n):
        return k_ref[pl.ds(pl.multiple_of(n * MOBA_BLOCK, MOBA_BLOCK), MOBA_BLOCK), :]

    def values(n):
        return jnp.concatenate([vt_ref[n], ones_rows], axis=0)

    def softmax(src_ref, j, h):
        s = src_ref[h]
        m_old = m_ref[j, h]
        m_new = jnp.maximum(m_old, jnp.max(s, axis=0, keepdims=True))
        m_ref[j, h] = m_new
        return j, h, jnp.exp2(s - m_new).astype(BF16), jnp.exp2(m_old - m_new)

    def accumulate(v_aug, j, h, e, alpha):
        pv = jnp.dot(v_aug, e, preferred_element_type=F32)
        acc_ref[j, h] = acc_ref[j, h] * alpha + pv

    def step(j, t, src_ref, dst_ref, next_own):
        if next_own:
            jn = jnp.minimum(j + 1, nblk - 1)
            kblk = key_block(jn)
        else:
            jn = j
            kblk = key_block(t)
        v_aug = values(jnp.where(t == 0, j, t - 1))
        pending = []
        for h in range(GROUP):
            s_next = jnp.dot(kblk, lhs_ref[jn, h], preferred_element_type=F32)
            dst_ref[h] = jnp.where(key <= qry, s_next, NEG_INF) if next_own else s_next
            pending.append(softmax(src_ref, j, h))
            if len(pending) > MOBA_PV_LAG:
                accumulate(v_aug, *pending.pop(0))
        for ready in pending:
            accumulate(v_aug, *ready)

    def query_block(j, cur_ref, other_ref, odd):
        def two_steps(i, carry):
            step(j, 2 * i, cur_ref, other_ref, False)
            step(j, 2 * i + 1, other_ref, cur_ref, False)
            return carry

        lax.fori_loop(0, j // 2, two_steps, 0)
        if odd:
            step(j, j - 1, cur_ref, other_ref, False)
            step(j, j, other_ref, cur_ref, True)
        else:
            step(j, j, cur_ref, other_ref, True)

    k_first = key_block(0)
    for h in range(GROUP):
        s0_ref[h] = jnp.where(key <= qry, jnp.dot(k_first, lhs_ref[0, h], preferred_element_type=F32),
                              NEG_INF)

    def four_query_blocks(i, carry):
        query_block(4 * i, s0_ref, s1_ref, False)
        query_block(4 * i + 1, s1_ref, s0_ref, True)
        query_block(4 * i + 2, s1_ref, s0_ref, False)
        query_block(4 * i + 3, s0_ref, s1_ref, True)
        return carry

    lax.fori_loop(0, nblk // 4, four_query_blocks, 0)

    def write_out(j, carry):
        rows = pl.ds(pl.multiple_of(j * MOBA_BLOCK, MOBA_BLOCK), MOBA_BLOCK)
        for p in range(PAIRS):
            pair = jnp.concatenate(
                [acc_ref[j, h, 0:HEAD_DIM, :] * (1.0 / acc_ref[j, h, HEAD_DIM:HEAD_DIM + 1, :])
                 for h in (2 * p, 2 * p + 1)], axis=0)
            o_ref[rows, p * PAIR_W:(p + 1) * PAIR_W] = pair.T.astype(o_ref.dtype)
        return carry

    lax.fori_loop(0, nblk, write_out, 0, unroll=2)


def moba_attention(q_t, k_aug, v_t, kmean, slope_rows, *, batch, seq):
    n = k_aug.shape[0]
    nblk = seq // MOBA_BLOCK
    assert nblk % 4 == 0
    lanes = GROUP * MOBA_BLOCK
    return pl.pallas_call(
        _moba_kernel,
        grid=(batch, N_KV_HEADS),
        in_specs=[pl.BlockSpec((1, SLOPE_ROWS, lanes), lambda b, kh: (kh, 0, 0)),
                  pl.BlockSpec((nblk, GROUP * HEAD_DIM, MOBA_BLOCK), lambda b, kh: (b, kh, 0)),
                  pl.BlockSpec((seq, PAIR_W), lambda b, kh: (b, kh)),
                  pl.BlockSpec((nblk, HEAD_DIM, MOBA_BLOCK), lambda b, kh: (b, kh, 0)),
                  pl.BlockSpec((1, nblk, PAIR_W), lambda b, kh: (b, 0, kh))],
        out_specs=pl.BlockSpec((seq, GROUP * HEAD_DIM), lambda b, kh: (b, kh)),
        out_shape=jax.ShapeDtypeStruct((n, Q_DIM), BF16),
        scratch_shapes=[pltpu.VMEM((nblk, GROUP, 2 * HEAD_DIM, MOBA_BLOCK), BF16),
                        pltpu.VMEM((nblk, GROUP, HEAD_DIM + ONES_ROWS, MOBA_BLOCK), F32),
                        pltpu.VMEM((nblk, GROUP, 1, MOBA_BLOCK), F32),
                        pltpu.VMEM((GROUP, MOBA_BLOCK, MOBA_BLOCK), F32),
                        pltpu.VMEM((GROUP, MOBA_BLOCK, MOBA_BLOCK), F32)],
        compiler_params=_cparams(("parallel", "arbitrary")),
        name="moba_attention",
    )(slope_rows, q_t, k_aug, v_t, kmean)


def _alibi_slopes_base2():
    slopes = np.exp2(-8.0 * np.arange(1, N_HEADS + 1, dtype=np.float32) / N_HEADS)
    return (slopes.astype(np.float32) * np.float32(LOG2E)).astype(np.float32)


def _slope_rows(slopes, width):
    hi = slopes.astype(BF16).astype(np.float32)
    mid = (slopes - hi).astype(BF16).astype(np.float32)
    lo = (slopes - hi - mid).astype(BF16).astype(np.float32)
    pieces = np.stack([hi, mid, lo], axis=-1)
    pad = np.zeros((N_HEADS, SLOPE_ROWS - 2 * SLOPE_PIECES), np.float32)
    rows = np.concatenate([pieces, pieces, pad], axis=1)
    rows = rows.reshape(N_KV_HEADS, GROUP, SLOPE_ROWS).transpose(0, 2, 1)
    return np.repeat(rows, width, axis=2)


def _key_features(seq):
    pos = np.arange(seq, dtype=np.int32)[:, None]
    blk = pos // MOBA_BLOCK
    lane = np.arange(PAIR_W, dtype=np.int32)[None, :] - HEAD_DIM
    feat = np.where(lane == blk, np.float32(NEG_INF), np.float32(0.0))
    feat = np.where((lane >= SLOPE_BLK_LANE) & (lane < SLOPE_POS_LANE),
                    (blk * MOBA_BLOCK).astype(np.float32), feat)
    feat = np.where((lane >= SLOPE_POS_LANE) & (lane < SLOPE_POS_LANE + SLOPE_PIECES),
                    (pos % MOBA_BLOCK).astype(np.float32), feat)
    return np.where(lane < 0, np.float32(0.0), feat).astype(np.float32)


def kernel(x, w_qkv_a, sinks_a, w_o_a, kv_norm, w_kv_shared, w_q_b, w_o_b,
           norm_attn_pre, norm_attn_post, norm_mlp_pre, norm_mlp_post, w_up, w_down):
    batch, seq, d = x.shape
    n = batch * seq
    assert seq // MOBA_BLOCK == SEL_LANES and seq % MOBA_BLOCK == 0
    qscale = HEAD_DIM ** -0.5 * LOG2E
    slopes2 = _alibi_slopes_base2()
    key_feat = _key_features(seq)

    wq_a, wk_a, wv_a = jnp.split(w_qkv_a[0], [Q_DIM, Q_DIM + KV_DIM], axis=-1)
    wk_s, wv_s = jnp.split(w_kv_shared, [KV_DIM], axis=-1)

    xf = x.reshape(n, d)

    q0_t, k0, v0_t = attn_proj(
        xf, norm_attn_pre[0], norm_attn_pre[0], (wq_a * qscale).T.astype(BF16),
        wk_a.astype(BF16), wv_a.T.astype(BF16), key_feat,
        tm=512, seq=seq, shared_norm=True, with_kmean=False, name="attn_proj0")
    sink_rows = jnp.repeat((sinks_a[0].astype(F32) * LOG2E).reshape(N_KV_HEADS, 1, GROUP), WINDOW, axis=2)
    mix0, (wup_all, wdn_all, wo_a, wo_b) = swa_attention(
        q0_t, k0, v0_t, _slope_rows(slopes2, WINDOW), sink_rows, [w_up, w_down, w_o_a, w_o_b],
        batch=batch, seq=seq)
    x1 = oproj_norm_residual(mix0, xf, norm_attn_post[0], wo_a[0], tm=512, name="oproj0")
    x2 = mlp_block(x1, norm_mlp_pre[0], norm_mlp_post[0], wup_all, wdn_all, 0, tm=512, tf=1024, name="mlp0")

    q1_t, k1, v1_t, kmean = attn_proj(
        x2, norm_attn_pre[1], kv_norm, (w_q_b[0] * qscale).T.astype(BF16),
        wk_s.astype(BF16), wv_s.T.astype(BF16), key_feat,
        tm=512, seq=seq, shared_norm=False, with_kmean=True, name="attn_proj1")
    kmean = kmean.reshape(batch, seq // MOBA_BLOCK, K_AUG).astype(BF16)
    mix1 = moba_attention(q1_t, k1, v1_t, kmean, _slope_rows(slopes2, MOBA_BLOCK), batch=batch, seq=seq)
    x3 = oproj_norm_residual(mix1, x2, norm_attn_post[1], wo_b[0], tm=512, name="oproj1")
    x4 = mlp_block(x3, norm_mlp_pre[1], norm_mlp_post[1], wup_all, wdn_all, 1, tm=512, tf=1024, name="mlp1")
    return x4.reshape(batch, seq, d)
```

```python
import functools

import jax
import jax.numpy as jnp
import numpy as np
from jax import lax
from jax.experimental import pallas as pl
from jax.experimental.pallas import tpu as pltpu

D_MODEL = 2048
N_HEADS = 32
N_KV_HEADS = 4
HEAD_DIM = 64
GROUP = N_HEADS // N_KV_HEADS
PAIRS = GROUP // 2
PAIR_W = 2 * HEAD_DIM
Q_DIM = N_HEADS * HEAD_DIM
KV_DIM = N_KV_HEADS * HEAD_DIM
K_AUG = N_KV_HEADS * PAIR_W
D_FF = 4 * D_MODEL
WINDOW = 128
MOBA_BLOCK = 256
MOBA_TOPK = 3
RMS_EPS = 1e-6
NEG_INF = -1e30
BELOW_NEG_INF = -3e38
RUNNING_MAX_INIT = -1e38
LOG2E = 1.4426950408889634

SEL_LANES = 16
SLOPE_BLK_LANE = 16
SLOPE_POS_LANE = 19
SLOPE_PIECES = 3
SLOPE_ROWS = 8
ONES_ROWS = 16
NORM_ROW_CHUNK = 8

VMEM_LIMIT = 56 * 1024 * 1024

BF16 = jnp.bfloat16
F32 = jnp.float32
NT_DIMS = (((1,), (1,)), ((), ()))


def _rms_scale(x):
    return lax.rsqrt(jnp.mean(x * x, axis=-1, keepdims=True) + RMS_EPS)


def _cparams(semantics):
    return pltpu.CompilerParams(dimension_semantics=semantics,
                                vmem_limit_bytes=VMEM_LIMIT)


def _ones_rows():
    row = lax.broadcasted_iota(jnp.int32, (ONES_ROWS, MOBA_BLOCK), 0)
    return jnp.where(row == 0, 1.0, 0.0).astype(BF16)


PROJ_CHUNK = 512


def _attn_proj_kernel(x_ref, gq_ref, gkv_ref, wqt_ref, wk_ref, wvt_ref, feat_ref,
                      qt_ref, k_ref, vt_ref, *km_refs, shared_norm):
    x = x_ref[...]
    xn = x * _rms_scale(x)
    hq = (xn * gq_ref[...]).astype(BF16)
    hkv = hq if shared_norm else (xn * gkv_ref[...]).astype(BF16)
    nblk = x_ref.shape[0] // MOBA_BLOCK
    for c in range(0, wqt_ref.shape[0], PROJ_CHUNK):
        qt = lax.dot_general(wqt_ref[c:c + PROJ_CHUNK, :], hq, NT_DIMS,
                             preferred_element_type=F32).astype(BF16)
        for blk in range(nblk):
            qt_ref[blk, c:c + PROJ_CHUNK, :] = qt[:, blk * MOBA_BLOCK:(blk + 1) * MOBA_BLOCK]
    kacc = jnp.dot(hkv, wk_ref[...], preferred_element_type=F32)
    feat = feat_ref[...]
    lane = lax.broadcasted_iota(jnp.int32, feat.shape, 1)
    for kh in range(N_KV_HEADS):
        two_heads = kacc[:, (kh // 2) * PAIR_W:(kh // 2 + 1) * PAIR_W]
        if kh % 2:
            two_heads = jnp.concatenate([two_heads[:, HEAD_DIM:], two_heads[:, :HEAD_DIM]], axis=1)
        k_head = jnp.where(lane < HEAD_DIM, two_heads, 0.0)
        cols = slice(kh * PAIR_W, (kh + 1) * PAIR_W)
        for km_ref in km_refs:
            for blk in range(nblk):
                rows = k_head[blk * MOBA_BLOCK:(blk + 1) * MOBA_BLOCK]
                km_ref[blk, :, cols] = jnp.sum(rows, axis=0, keepdims=True) * (1.0 / MOBA_BLOCK)
        k_ref[:, cols] = (k_head + feat).astype(BF16)
    vt = lax.dot_general(wvt_ref[...], hkv, NT_DIMS, preferred_element_type=F32)
    for blk in range(nblk):
        vt_ref[blk] = vt[:, blk * MOBA_BLOCK:(blk + 1) * MOBA_BLOCK].astype(BF16)


def attn_proj(x, gq, gkv, wq_t, wk, wv_t, key_feat, *, tm, seq, shared_norm, with_kmean, name):
    n, d = x.shape
    nb = tm // MOBA_BLOCK
    tiles_per_seq = seq // tm
    whole = lambda w: pl.BlockSpec(w.shape, lambda i: (0, 0))
    gain = pl.BlockSpec((1, d), lambda i: (0, 0))
    out_specs = [pl.BlockSpec((nb, wq_t.shape[0], MOBA_BLOCK), lambda i: (i, 0, 0)),
                 pl.BlockSpec((tm, K_AUG), lambda i: (i, 0)),
                 pl.BlockSpec((nb, KV_DIM, MOBA_BLOCK), lambda i: (i, 0, 0))]
    out_shape = [jax.ShapeDtypeStruct((n // MOBA_BLOCK, wq_t.shape[0], MOBA_BLOCK), BF16),
                 jax.ShapeDtypeStruct((n, K_AUG), BF16),
                 jax.ShapeDtypeStruct((n // MOBA_BLOCK, KV_DIM, MOBA_BLOCK), BF16)]
    if with_kmean:
        out_specs.append(pl.BlockSpec((nb, 1, K_AUG), lambda i: (i, 0, 0)))
        out_shape.append(jax.ShapeDtypeStruct((n // MOBA_BLOCK, 1, K_AUG), F32))
    return pl.pallas_call(
        functools.partial(_attn_proj_kernel, shared_norm=shared_norm),
        grid=(n // tm,),
        in_specs=[pl.BlockSpec((tm, d), lambda i: (i, 0)), gain, gain,
                  whole(wq_t), whole(wk), whole(wv_t),
                  pl.BlockSpec((tm, PAIR_W), lambda i: (i % tiles_per_seq, 0))],
        out_specs=out_specs,
        out_shape=out_shape,
        compiler_params=_cparams(("parallel",)),
        name=name,
    )(x, gq.reshape(1, d), gkv.reshape(1, d), wq_t, wk, wv_t, key_feat)


def _oproj_kernel(mix_ref, x_ref, g_ref, w_ref, o_ref):
    a = jnp.dot(mix_ref[...], w_ref[...], preferred_element_type=F32)
    o_ref[...] = x_ref[...] + (a * _rms_scale(a)) * g_ref[...]


def oproj_norm_residual(mix, x, g, w, *, tm, name):
    n, d = x.shape
    return pl.pallas_call(
        _oproj_kernel,
        grid=(n // tm,),
        in_specs=[pl.BlockSpec((tm, mix.shape[1]), lambda i: (i, 0)),
                  pl.BlockSpec((tm, d), lambda i: (i, 0)),
                  pl.BlockSpec((1, d), lambda i: (0, 0)),
                  pl.BlockSpec(w.shape, lambda i: (0, 0))],
        out_specs=pl.BlockSpec((tm, d), lambda i: (i, 0)),
        out_shape=jax.ShapeDtypeStruct((n, d), F32),
        compiler_params=_cparams(("parallel",)),
        name=name,
    )(mix, x, g.reshape(1, d), w)


MLP_FF_CHUNK = 512


def _mlp_kernel(x_ref, gpre_ref, gpost_ref, wup_ref, wdn_ref, o_ref, h_ref, acc_ref):
    f = pl.program_id(1)

    @pl.when(f == 0)
    def _():
        x = x_ref[...]
        h_ref[...] = ((x * _rms_scale(x)) * gpre_ref[...]).astype(BF16)
        acc_ref[...] = jnp.zeros(acc_ref.shape, F32)

    h = h_ref[...]
    chunks = wup_ref.shape[1] // MLP_FF_CHUNK

    def up(c):
        cols = slice(c * MLP_FF_CHUNK, (c + 1) * MLP_FF_CHUNK)
        return jnp.dot(h, wup_ref[:, cols], preferred_element_type=F32)

    u_next = up(0)
    for c in range(chunks):
        u = jnp.maximum(u_next, 0.0)
        if c + 1 < chunks:
            u_next = up(c + 1)
        rows = slice(c * MLP_FF_CHUNK, (c + 1) * MLP_FF_CHUNK)
        acc_ref[...] += jnp.dot((u * u).astype(BF16), wdn_ref[rows, :], preferred_element_type=F32)

    @pl.when(f == pl.num_programs(1) - 1)
    def _():
        for r in range(0, acc_ref.shape[0], NORM_ROW_CHUNK):
            a = acc_ref[r:r + NORM_ROW_CHUNK, :]
            o_ref[r:r + NORM_ROW_CHUNK, :] = (
                x_ref[r:r + NORM_ROW_CHUNK, :] + (a * _rms_scale(a)) * gpost_ref[...])


def mlp_block(x, gpre, gpost, wup_all, wdn_all, layer, *, tm, tf, name):
    n, d = x.shape
    dff = wup_all.shape[2]
    return pl.pallas_call(
        _mlp_kernel,
        grid=(n // tm, dff // tf),
        in_specs=[pl.BlockSpec((tm, d), lambda i, f: (i, 0)),
                  pl.BlockSpec((1, d), lambda i, f: (0, 0)),
                  pl.BlockSpec((1, d), lambda i, f: (0, 0)),
                  pl.BlockSpec((None, d, tf), lambda i, f: (layer, 0, f)),
                  pl.BlockSpec((None, tf, d), lambda i, f: (layer, f, 0))],
        out_specs=pl.BlockSpec((tm, d), lambda i, f: (i, 0)),
        out_shape=jax.ShapeDtypeStruct((n, d), F32),
        scratch_shapes=[pltpu.VMEM((tm, d), BF16), pltpu.VMEM((tm, d), F32)],
        compiler_params=_cparams(("parallel", "arbitrary")),
        name=name,
    )(x, gpre.reshape(1, d), gpost.reshape(1, d), wup_all, wdn_all)


SWA_PV_LAG = 6
SWA_BLOCKS_PER_STEP = 2


def _swa_kernel(slope_ref, sink_ref, q_ref, k_ref, vt_ref, *refs, n_cast):
    o_ref, lhs_ref = refs[n_cast], refs[-1]
    for src_ref, dst_ref in zip(refs[:n_cast], refs[n_cast + 1:2 * n_cast + 1]):
        dst_ref[...] = src_ref[...].astype(BF16)

    half_lanes = GROUP * WINDOW
    zeros_top = jnp.zeros((SEL_LANES, half_lanes), F32)
    zeros_bottom = jnp.zeros((HEAD_DIM - SEL_LANES - SLOPE_ROWS, half_lanes), F32)
    ones_rows = _ones_rows()
    key = lax.broadcasted_iota(jnp.int32, (MOBA_BLOCK, 2 * WINDOW), 0)
    qry = lax.broadcasted_iota(jnp.int32, (MOBA_BLOCK, 2 * WINDOW), 1) & (WINDOW - 1)
    lane_q = lax.broadcasted_iota(jnp.int32, (1, half_lanes), 1) & (WINDOW - 1)
    slopes = [slope_ref[kh, 0:1, :] + slope_ref[kh, 1:2, :] + slope_ref[kh, 2:3, :]
              for kh in range(N_KV_HEADS)]

    base, v_half, keep, sinks = {}, {}, {}, {}
    for blk in range(SWA_BLOCKS_PER_STEP):
        j = pl.program_id(1) * SWA_BLOCKS_PER_STEP + blk
        for kh in range(N_KV_HEADS):
            feat = jnp.concatenate([zeros_top, slope_ref[kh], zeros_bottom], axis=0).astype(BF16)
            for half in range(2):
                for h in range(GROUP):
                    row0 = (kh * GROUP + h) * HEAD_DIM
                    lhs_ref[blk, kh, half, 0:HEAD_DIM, h * WINDOW:(h + 1) * WINDOW] = (
                        q_ref[blk, row0:row0 + HEAD_DIM, half * WINDOW:(half + 1) * WINDOW])
                lhs_ref[blk, kh, half, HEAD_DIM:2 * HEAD_DIM, :] = feat

        base[blk] = [jnp.maximum(j * MOBA_BLOCK - WINDOW, 0), j * MOBA_BLOCK]
        v_own = vt_ref[j]
        v_prev = vt_ref[jnp.maximum(j - 1, 0)]
        v_straddle = jnp.concatenate([v_prev[:, WINDOW:], v_own[:, :WINDOW]], axis=1)
        v_half[blk] = [jnp.where(j > 0, v_straddle, v_own), v_own]
        for half in range(2):
            dist = (j * MOBA_BLOCK + half * WINDOW - base[blk][half]) + qry - key
            keep[(blk, half)] = (dist & -WINDOW) == 0
            pos = (j * MOBA_BLOCK + half * WINDOW + lane_q).astype(F32)
            for kh in range(N_KV_HEADS):
                sinks[(blk, kh, half)] = sink_ref[kh] + slopes[kh] * pos

    def keys(blk, kh, half):
        start = pl.multiple_of(base[blk][half], WINDOW)
        return k_ref[pl.ds(start, MOBA_BLOCK), kh * PAIR_W:(kh + 1) * PAIR_W]

    def values(blk, kh, half):
        return jnp.concatenate([v_half[blk][half][kh * HEAD_DIM:(kh + 1) * HEAD_DIM, :], ones_rows], axis=0)

    items = [(blk, kh, half, c) for blk in range(SWA_BLOCKS_PER_STEP) for kh in range(N_KV_HEADS)
             for c in range(PAIRS) for half in range(2)]
    raw, pvs = {}, {}

    def pair_cols(c):
        return slice(c * 2 * WINDOW, (c + 1) * 2 * WINDOW)

    def scores(i):
        blk, kh, half, c = items[i]
        raw[i] = jnp.dot(keys(blk, kh, half), lhs_ref[blk, kh, half, :, pair_cols(c)],
                         preferred_element_type=F32)

    def softmax_pv(i):
        blk, kh, half, c = items[i]
        s = jnp.where(keep[(blk, half)], raw.pop(i), NEG_INF)
        sink = sinks[(blk, kh, half)][:, pair_cols(c)]
        m = jnp.maximum(jnp.max(s, axis=0, keepdims=True), sink)
        e = jnp.exp2(s - m).astype(BF16)
        pv = jnp.dot(values(blk, kh, half), e, preferred_element_type=F32)
        pvs[i] = (pv, jnp.exp2(sink - m))

    def normalise(i):
        pv, sink_term = pvs.pop(i)
        l = pv[HEAD_DIM:HEAD_DIM + 1, :] + sink_term
        return pv[0:HEAD_DIM, :] * (1.0 / l)

    done = {}
    for i in range(len(items) + SWA_PV_LAG + 1):
        if i < len(items):
            scores(i)
        if 0 <= i - SWA_PV_LAG < len(items):
            softmax_pv(i - SWA_PV_LAG)
        n = i - SWA_PV_LAG - 1
        if 0 <= n < len(items):
            blk, kh, half, c = items[n]
            done[half] = normalise(n)
            if half == 1:
                pair = jnp.concatenate(
                    [jnp.concatenate([done[0][:, :WINDOW], done[1][:, :WINDOW]], axis=1),
                     jnp.concatenate([done[0][:, WINDOW:], done[1][:, WINDOW:]], axis=1)],
                    axis=0)
                col0 = (kh * PAIRS + c) * PAIR_W
                o_ref[blk * MOBA_BLOCK:(blk + 1) * MOBA_BLOCK, col0:col0 + PAIR_W] = (
                    pair.T.astype(o_ref.dtype))


def swa_attention(q_t, k_aug, v_t, slope_rows, sink_rows, to_bf16, *, batch, seq):
    n = k_aug.shape[0]
    nblk = seq // MOBA_BLOCK
    half_lanes = GROUP * WINDOW
    per_seq = nblk // SWA_BLOCKS_PER_STEP
    steps = batch * per_seq
    cast_specs = [pl.BlockSpec((w.shape[0], w.shape[1] // steps, w.shape[2]),
                               lambda b, j: (0, b * per_seq + j, 0)) for w in to_bf16]
    outs = pl.pallas_call(
        functools.partial(_swa_kernel, n_cast=len(to_bf16)),
        grid=(batch, per_seq),
        in_specs=[pl.BlockSpec(slope_rows.shape, lambda b, j: (0, 0, 0)),
                  pl.BlockSpec(sink_rows.shape, lambda b, j: (0, 0, 0)),
                  pl.BlockSpec((SWA_BLOCKS_PER_STEP, Q_DIM, MOBA_BLOCK), lambda b, j: (b * per_seq + j, 0, 0)),
                  pl.BlockSpec((seq, K_AUG), lambda b, j: (b, 0)),
                  pl.BlockSpec((nblk, KV_DIM, MOBA_BLOCK), lambda b, j: (b, 0, 0))] + cast_specs,
        out_specs=[pl.BlockSpec((SWA_BLOCKS_PER_STEP * MOBA_BLOCK, Q_DIM),
                                lambda b, j: (b * per_seq + j, 0))] + cast_specs,
        out_shape=[jax.ShapeDtypeStruct((n, Q_DIM), BF16)]
                  + [jax.ShapeDtypeStruct(w.shape, BF16) for w in to_bf16],
        scratch_shapes=[pltpu.VMEM((SWA_BLOCKS_PER_STEP, N_KV_HEADS, 2, 2 * HEAD_DIM, half_lanes), BF16)],
        compiler_params=_cparams(("parallel", "arbitrary")),
        name="swa_attention",
    )(slope_rows, sink_rows, q_t, k_aug, v_t, *to_bf16)
    return outs[0], outs[1:]


MOBA_PV_LAG = 2


def _moba_kernel(slope_ref, q_ref, k_ref, vt_ref, km_ref, o_ref,
                 lhs_ref, acc_ref, m_ref, s0_ref, s1_ref):
    lanes = GROUP * MOBA_BLOCK
    nblk = km_ref.shape[1]
    km = km_ref[0][:, :HEAD_DIM]
    blk = lax.broadcasted_iota(jnp.int32, (nblk, lanes), 0)
    blkf = blk.astype(F32)
    pad = jnp.zeros((HEAD_DIM - SEL_LANES - SLOPE_ROWS, lanes), F32)

    def build_operand(j, carry):
        qt = q_ref[j]
        gate = jnp.concatenate(
            [jnp.dot(km, qt[h * HEAD_DIM:(h + 1) * HEAD_DIM, :], preferred_element_type=F32)
             for h in range(GROUP)], axis=1)
        g = jnp.where(blk < j, gate, NEG_INF)
        sel = jnp.zeros((nblk, lanes), F32)
        for _ in range(MOBA_TOPK):
            mx = jnp.max(g, axis=0, keepdims=True)
            idx = jnp.min(jnp.where(g == mx, blkf, float(nblk)), axis=0, keepdims=True)
            hit = blkf == idx
            sel = jnp.where(hit, 1.0, sel)
            g = jnp.where(hit, BELOW_NEG_INF, g)
        not_sel = jnp.where(blk < j, 1.0 - sel, jnp.where(blk == j, 0.0, 1.0))
        feat = jnp.concatenate([not_sel, slope_ref[0], pad], axis=0).astype(BF16)
        for h in range(GROUP):
            lhs_ref[j, h, 0:HEAD_DIM, :] = qt[h * HEAD_DIM:(h + 1) * HEAD_DIM, :]
            lhs_ref[j, h, HEAD_DIM:2 * HEAD_DIM, :] = feat[:, h * MOBA_BLOCK:(h + 1) * MOBA_BLOCK]
        return carry

    lax.fori_loop(0, nblk, build_operand, 0, unroll=4)
    m_ref[...] = jnp.full(m_ref.shape, RUNNING_MAX_INIT, F32)
    acc_ref[...] = jnp.zeros(acc_ref.shape, F32)

    ones_rows = _ones_rows()
    key = lax.broadcasted_iota(jnp.int32, (MOBA_BLOCK, MOBA_BLOCK), 0)
    qry = lax.broadcasted_iota(jnp.int32, (MOBA_BLOCK, MOBA_BLOCK), 1)

    def key_block(n):
        return k_ref[pl.ds(pl.multiple_of(n * MOBA_BLOCK, MOBA_BLOCK), MOBA_BLOCK), :]

    def values(n):
        return jnp.concatenate([vt_ref[n], ones_rows], axis=0)

    def softmax(src_ref, j, h):
        s = src_ref[h]
        m_old = m_ref[j, h]
        m_new = jnp.maximum(m_old, jnp.max(s, axis=0, keepdims=True))
        m_ref[j, h] = m_new
        return j, h, jnp.exp2(s - m_new).astype(BF16), jnp.exp2(m_old - m_new)

    def accumulate(v_aug, j, h, e, alpha):
        pv = jnp.dot(v_aug, e, preferred_element_type=F32)
        acc_ref[j, h] = acc_ref[j, h] * alpha + pv

    def step(j, t, src_ref, dst_ref, next_own):
        if next_own:
            jn = jnp.minimum(j + 1, nblk - 1)
            kblk = key_block(jn)
        else:
            jn = j
            kblk = key_block(t)
        v_aug = values(jnp.where(t == 0, j, t - 1))
        pending = []
        for h in range(GROUP):
            s_next = jnp.dot(kblk, lhs_ref[jn, h], preferred_element_type=F32)
            dst_ref[h] = jnp.where(key <= qry, s_next, NEG_INF) if next_own else s_next
            pending.append(softmax(src_ref, j, h))
            if len(pending) > MOBA_PV_LAG:
                accumulate(v_aug, *pending.pop(0))
        for ready in pending:
            accumulate(v_aug, *ready)

    def query_block(j, cur_ref, other_ref, odd):
        def two_steps(i, carry):
            step(j, 2 * i, cur_ref, other_ref, False)
            step(j, 2 * i + 1, other_ref, cur_ref, False)
            return carry

        lax.fori_loop(0, j // 2, two_steps, 0)
        if odd:
            step(j, j - 1, cur_ref, other_ref, False)
            step(j, j, other_ref, cur_ref, True)
        else:
            step(j, j, cur_ref, other_ref, True)

    k_first = key_block(0)
    for h in range(GROUP):
        s0_ref[h] = jnp.where(key <= qry, jnp.dot(k_first, lhs_ref[0, h], preferred_element_type=F32),
                              NEG_INF)

    def four_query_blocks(i, carry):
        query_block(4 * i, s0_ref, s1_ref, False)
        query_block(4 * i + 1, s1_ref, s0_ref, True)
        query_block(4 * i + 2, s1_ref, s0_ref, False)
        query_block(4 * i + 3, s0_ref, s1_ref, True)
        return carry

    lax.fori_loop(0, nblk // 4, four_query_blocks, 0)

    def write_out(j, carry):
        rows = pl.ds(pl.multiple_of(j * MOBA_BLOCK, MOBA_BLOCK), MOBA_BLOCK)
        for p in range(PAIRS):
            pair = jnp.concatenate(
                [acc_ref[j, h, 0:HEAD_DIM, :] * (1.0 / acc_ref[j, h, HEAD_DIM:HEAD_DIM + 1, :])
                 for h in (2 * p, 2 * p + 1)], axis=0)
            o_ref[rows, p * PAIR_W:(p + 1) * PAIR_W] = pair.T.astype(o_ref.dtype)
        return carry

    lax.fori_loop(0, nblk, write_out, 0, unroll=2)


def moba_attention(q_t, k_aug, v_t, kmean, slope_rows, *, batch, seq):
    n = k_aug.shape[0]
    nblk = seq // MOBA_BLOCK
    assert nblk % 4 == 0
    lanes = GROUP * MOBA_BLOCK
    return pl.pallas_call(
        _moba_kernel,
        grid=(batch, N_KV_HEADS),
        in_specs=[pl.BlockSpec((1, SLOPE_ROWS, lanes), lambda b, kh: (kh, 0, 0)),
                  pl.BlockSpec((nblk, GROUP * HEAD_DIM, MOBA_BLOCK), lambda b, kh: (b, kh, 0)),
                  pl.BlockSpec((seq, PAIR_W), lambda b, kh: (b, kh)),
                  pl.BlockSpec((nblk, HEAD_DIM, MOBA_BLOCK), lambda b, kh: (b, kh, 0)),
                  pl.BlockSpec((1, nblk, PAIR_W), lambda b, kh: (b, 0, kh))],
        out_specs=pl.BlockSpec((seq, GROUP * HEAD_DIM), lambda b, kh: (b, kh)),
        out_shape=jax.ShapeDtypeStruct((n, Q_DIM), BF16),
        scratch_shapes=[pltpu.VMEM((nblk, GROUP, 2 * HEAD_DIM, MOBA_BLOCK), BF16),
                        pltpu.VMEM((nblk, GROUP, HEAD_DIM + ONES_ROWS, MOBA_BLOCK), F32),
                        pltpu.VMEM((nblk, GROUP, 1, MOBA_BLOCK), F32),
                        pltpu.VMEM((GROUP, MOBA_BLOCK, MOBA_BLOCK), F32),
                        pltpu.VMEM((GROUP, MOBA_BLOCK, MOBA_BLOCK), F32)],
        compiler_params=_cparams(("parallel", "arbitrary")),
        name="moba_attention",
    )(slope_rows, q_t, k_aug, v_t, kmean)


def _alibi_slopes_base2():
    slopes = np.exp2(-8.0 * np.arange(1, N_HEADS + 1, dtype=np.float32) / N_HEADS)
    return (slopes.astype(np.float32) * np.float32(LOG2E)).astype(np.float32)


def _slope_rows(slopes, width):
    hi = slopes.astype(BF16).astype(np.float32)
    mid = (slopes - hi).astype(BF16).astype(np.float32)
    lo = (slopes - hi - mid).astype(BF16).astype(np.float32)
    pieces = np.stack([hi, mid, lo], axis=-1)
    pad = np.zeros((N_HEADS, SLOPE_ROWS - 2 * SLOPE_PIECES), np.float32)
    rows = np.concatenate([pieces, pieces, pad], axis=1)
    rows = rows.reshape(N_KV_HEADS, GROUP, SLOPE_ROWS).transpose(0, 2, 1)
    return np.repeat(rows, width, axis=2)


def _key_features(seq):
    pos = np.arange(seq, dtype=np.int32)[:, None]
    blk = pos // MOBA_BLOCK
    lane = np.arange(PAIR_W, dtype=np.int32)[None, :] - HEAD_DIM
    feat = np.where(lane == blk, np.float32(NEG_INF), np.float32(0.0))
    feat = np.where((lane >= SLOPE_BLK_LANE) & (lane < SLOPE_POS_LANE),
                    (blk * MOBA_BLOCK).astype(np.float32), feat)
    feat = np.where((lane >= SLOPE_POS_LANE) & (lane < SLOPE_POS_LANE + SLOPE_PIECES),
                    (pos % MOBA_BLOCK).astype(np.float32), feat)
    return np.where(lane < 0, np.float32(0.0), feat).astype(np.float32)


def kernel(x, w_qkv_a, sinks_a, w_o_a, kv_norm, w_kv_shared, w_q_b, w_o_b,
           norm_attn_pre, norm_attn_post, norm_mlp_pre, norm_mlp_post, w_up, w_down):
    batch, seq, d = x.shape
    n = batch * seq
    assert seq // MOBA_BLOCK == SEL_LANES and seq % MOBA_BLOCK == 0
    qscale = HEAD_DIM ** -0.5 * LOG2E
    slopes2 = _alibi_slopes_base2()
    key_feat = _key_features(seq)

    wq_a, wk_a, wv_a = jnp.split(w_qkv_a[0], [Q_DIM, Q_DIM + KV_DIM], axis=-1)
    wk_s, wv_s = jnp.split(w_kv_shared, [KV_DIM], axis=-1)

    xf = x.reshape(n, d)

    q0_t, k0, v0_t = attn_proj(
        xf, norm_attn_pre[0], norm_attn_pre[0], (wq_a * qscale).T.astype(BF16),
        wk_a.astype(BF16), wv_a.T.astype(BF16), key_feat,
        tm=512, seq=seq, shared_norm=True, with_kmean=False, name="attn_proj0")
    sink_rows = jnp.repeat((sinks_a[0].astype(F32) * LOG2E).reshape(N_KV_HEADS, 1, GROUP), WINDOW, axis=2)
    mix0, (wup_all, wdn_all, wo_a, wo_b) = swa_attention(
        q0_t, k0, v0_t, _slope_rows(slopes2, WINDOW), sink_rows, [w_up, w_down, w_o_a, w_o_b],
        batch=batch, seq=seq)
    x1 = oproj_norm_residual(mix0, xf, norm_attn_post[0], wo_a[0], tm=512, name="oproj0")
    x2 = mlp_block(x1, norm_mlp_pre[0], norm_mlp_post[0], wup_all, wdn_all, 0, tm=512, tf=1024, name="mlp0")

    q1_t, k1, v1_t, kmean = attn_proj(
        x2, norm_attn_pre[1], kv_norm, (w_q_b[0] * qscale).T.astype(BF16),
        wk_s.astype(BF16), wv_s.T.astype(BF16), key_feat,
        tm=512, seq=seq, shared_norm=False, with_kmean=True, name="attn_proj1")
    kmean = kmean.reshape(batch, seq // MOBA_BLOCK, K_AUG).astype(BF16)
    mix1 = moba_attention(q1_t, k1, v1_t, kmean, _slope_rows(slopes2, MOBA_BLOCK), batch=batch, seq=seq)
    x3 = oproj_norm_residual(mix1, x2, norm_attn_post[1], wo_b[0], tm=512, name="oproj1")
    x4 = mlp_block(x3, norm_mlp_pre[1], norm_mlp_post[1], wup_all, wdn_all, 1, tm=512, tf=1024, name="mlp1")
    return x4.reshape(batch, seq, d)
```

```python
import functools

import jax
import jax.numpy as jnp
import numpy as np
from jax import lax
from jax.experimental import pallas as pl
from jax.experimental.pallas import tpu as pltpu

D_MODEL = 2048
N_HEADS = 32
N_KV_HEADS = 4
HEAD_DIM = 64
GROUP = N_HEADS // N_KV_HEADS
PAIRS = GROUP // 2
PAIR_W = 2 * HEAD_DIM
Q_DIM = N_HEADS * HEAD_DIM
KV_DIM = N_KV_HEADS * HEAD_DIM
K_AUG = N_KV_HEADS * PAIR_W
D_FF = 4 * D_MODEL
WINDOW = 128
MOBA_BLOCK = 256
MOBA_TOPK = 3
RMS_EPS = 1e-6
NEG_INF = -1e30
BELOW_NEG_INF = -3e38
RUNNING_MAX_INIT = -1e38
LOG2E = 1.4426950408889634

SEL_LANES = 16
SLOPE_BLK_LANE = 16
SLOPE_POS_LANE = 19
SLOPE_PIECES = 3
SLOPE_ROWS = 8
ONES_ROWS = 16
NORM_ROW_CHUNK = 8

VMEM_LIMIT = 56 * 1024 * 1024

BF16 = jnp.bfloat16
F32 = jnp.float32
NT_DIMS = (((1,), (1,)), ((), ()))


def _rms_scale(x):
    return lax.rsqrt(jnp.mean(x * x, axis=-1, keepdims=True) + RMS_EPS)


def _cparams(semantics):
    return pltpu.CompilerParams(dimension_semantics=semantics,
                                vmem_limit_bytes=VMEM_LIMIT)


def _ones_rows():
    row = lax.broadcasted_iota(jnp.int32, (ONES_ROWS, MOBA_BLOCK), 0)
    return jnp.where(row == 0, 1.0, 0.0).astype(BF16)


PROJ_CHUNK = 512


def _attn_proj_kernel(x_ref, gq_ref, gkv_ref, wqt_ref, wk_ref, wvt_ref, feat_ref,
                      qt_ref, k_ref, vt_ref, *km_refs, shared_norm):
    x = x_ref[...]
    xn = x * _rms_scale(x)
    hq = (xn * gq_ref[...]).astype(BF16)
    hkv = hq if shared_norm else (xn * gkv_ref[...]).astype(BF16)
    nblk = x_ref.shape[0] // MOBA_BLOCK
    for c in range(0, wqt_ref.shape[0], PROJ_CHUNK):
        qt = lax.dot_general(wqt_ref[c:c + PROJ_CHUNK, :], hq, NT_DIMS,
                             preferred_element_type=F32).astype(BF16)
        for blk in range(nblk):
            qt_ref[blk, c:c + PROJ_CHUNK, :] = qt[:, blk * MOBA_BLOCK:(blk + 1) * MOBA_BLOCK]
    kacc = jnp.dot(hkv, wk_ref[...], preferred_element_type=F32)
    feat = feat_ref[...]
    lane = lax.broadcasted_iota(jnp.int32, feat.shape, 1)
    for kh in range(N_KV_HEADS):
        two_heads = kacc[:, (kh // 2) * PAIR_W:(kh // 2 + 1) * PAIR_W]
        if kh % 2:
            two_heads = jnp.concatenate([two_heads[:, HEAD_DIM:], two_heads[:, :HEAD_DIM]], axis=1)
        k_head = jnp.where(lane < HEAD_DIM, two_heads, 0.0)
        cols = slice(kh * PAIR_W, (kh + 1) * PAIR_W)
        for km_ref in km_refs:
            for blk in range(nblk):
                rows = k_head[blk * MOBA_BLOCK:(blk + 1) * MOBA_BLOCK]
                km_ref[blk, :, cols] = jnp.sum(rows, axis=0, keepdims=True) * (1.0 / MOBA_BLOCK)
        k_ref[:, cols] = (k_head + feat).astype(BF16)
    vt = lax.dot_general(wvt_ref[...], hkv, NT_DIMS, preferred_element_type=F32)
    for blk in range(nblk):
        vt_ref[blk] = vt[:, blk * MOBA_BLOCK:(blk + 1) * MOBA_BLOCK].astype(BF16)


def attn_proj(x, gq, gkv, wq_t, wk, wv_t, key_feat, *, tm, seq, shared_norm, with_kmean, name):
    n, d = x.shape
    nb = tm // MOBA_BLOCK
    tiles_per_seq = seq // tm
    whole = lambda w: pl.BlockSpec(w.shape, lambda i: (0, 0))
    gain = pl.BlockSpec((1, d), lambda i: (0, 0))
    out_specs = [pl.BlockSpec((nb, wq_t.shape[0], MOBA_BLOCK), lambda i: (i, 0, 0)),
                 pl.BlockSpec((tm, K_AUG), lambda i: (i, 0)),
                 pl.BlockSpec((nb, KV_DIM, MOBA_BLOCK), lambda i: (i, 0, 0))]
    out_shape = [jax.ShapeDtypeStruct((n // MOBA_BLOCK, wq_t.shape[0], MOBA_BLOCK), BF16),
                 jax.ShapeDtypeStruct((n, K_AUG), BF16),
                 jax.ShapeDtypeStruct((n // MOBA_BLOCK, KV_DIM, MOBA_BLOCK), BF16)]
    if with_kmean:
        out_specs.append(pl.BlockSpec((nb, 1, K_AUG), lambda i: (i, 0, 0)))
        out_shape.append(jax.ShapeDtypeStruct((n // MOBA_BLOCK, 1, K_AUG), F32))
    return pl.pallas_call(
        functools.partial(_attn_proj_kernel, shared_norm=shared_norm),
        grid=(n // tm,),
        in_specs=[pl.BlockSpec((tm, d), lambda i: (i, 0)), gain, gain,
                  whole(wq_t), whole(wk), whole(wv_t),
                  pl.BlockSpec((tm, PAIR_W), lambda i: (i % tiles_per_seq, 0))],
        out_specs=out_specs,
        out_shape=out_shape,
        compiler_params=_cparams(("parallel",)),
        name=name,
    )(x, gq.reshape(1, d), gkv.reshape(1, d), wq_t, wk, wv_t, key_feat)


def _oproj_kernel(mix_ref, x_ref, g_ref, w_ref, o_ref):
    a = jnp.dot(mix_ref[...], w_ref[...], preferred_element_type=F32)
    o_ref[...] = x_ref[...] + (a * _rms_scale(a)) * g_ref[...]


def oproj_norm_residual(mix, x, g, w, *, tm, name):
    n, d = x.shape
    return pl.pallas_call(
        _oproj_kernel,
        grid=(n // tm,),
        in_specs=[pl.BlockSpec((tm, mix.shape[1]), lambda i: (i, 0)),
                  pl.BlockSpec((tm, d), lambda i: (i, 0)),
                  pl.BlockSpec((1, d), lambda i: (0, 0)),
                  pl.BlockSpec(w.shape, lambda i: (0, 0))],
        out_specs=pl.BlockSpec((tm, d), lambda i: (i, 0)),
        out_shape=jax.ShapeDtypeStruct((n, d), F32),
        compiler_params=_cparams(("parallel",)),
        name=name,
    )(mix, x, g.reshape(1, d), w)


MLP_FF_CHUNK = 512


def _mlp_kernel(x_ref, gpre_ref, gpost_ref, wup_ref, wdn_ref, o_ref, h_ref, acc_ref):
    f = pl.program_id(1)

    @pl.when(f == 0)
    def _():
        x = x_ref[...]
        h_ref[...] = ((x * _rms_scale(x)) * gpre_ref[...]).astype(BF16)
        acc_ref[...] = jnp.zeros(acc_ref.shape, F32)

    h = h_ref[...]
    chunks = wup_ref.shape[1] // MLP_FF_CHUNK

    def up(c):
        cols = slice(c * MLP_FF_CHUNK, (c + 1) * MLP_FF_CHUNK)
        return jnp.dot(h, wup_ref[:, cols], preferred_element_type=F32)

    u_next = up(0)
    for c in range(chunks):
        u = jnp.maximum(u_next, 0.0)
        if c + 1 < chunks:
            u_next = up(c + 1)
        rows = slice(c * MLP_FF_CHUNK, (c + 1) * MLP_FF_CHUNK)
        acc_ref[...] += jnp.dot((u * u).astype(BF16), wdn_ref[rows, :], preferred_element_type=F32)

    @pl.when(f == pl.num_programs(1) - 1)
    def _():
        for r in range(0, acc_ref.shape[0], NORM_ROW_CHUNK):
            a = acc_ref[r:r + NORM_ROW_CHUNK, :]
            o_ref[r:r + NORM_ROW_CHUNK, :] = (
                x_ref[r:r + NORM_ROW_CHUNK, :] + (a * _rms_scale(a)) * gpost_ref[...])


def mlp_block(x, gpre, gpost, wup_all, wdn_all, layer, *, tm, tf, name):
    n, d = x.shape
    dff = wup_all.shape[2]
    return pl.pallas_call(
        _mlp_kernel,
        grid=(n // tm, dff // tf),
        in_specs=[pl.BlockSpec((tm, d), lambda i, f: (i, 0)),
                  pl.BlockSpec((1, d), lambda i, f: (0, 0)),
                  pl.BlockSpec((1, d), lambda i, f: (0, 0)),
                  pl.BlockSpec((None, d, tf), lambda i, f: (layer, 0, f)),
                  pl.BlockSpec((None, tf, d), lambda i, f: (layer, f, 0))],
        out_specs=pl.BlockSpec((tm, d), lambda i, f: (i, 0)),
        out_shape=jax.ShapeDtypeStruct((n, d), F32),
        scratch_shapes=[pltpu.VMEM((tm, d), BF16), pltpu.VMEM((tm, d), F32)],
        compiler_params=_cparams(("parallel", "arbitrary")),
        name=name,
    )(x, gpre.reshape(1, d), gpost.reshape(1, d), wup_all, wdn_all)


SWA_PV_LAG = 6
SWA_BLOCKS_PER_STEP = 2


def _swa_kernel(slope_ref, sink_ref, q_ref, k_ref, vt_ref, *refs, n_cast):
    o_ref, lhs_ref = refs[n_cast], refs[-1]
    for src_ref, dst_ref in zip(refs[:n_cast], refs[n_cast + 1:2 * n_cast + 1]):
        dst_ref[...] = src_ref[...].astype(BF16)

    half_lanes = GROUP * WINDOW
    zeros_top = jnp.zeros((SEL_LANES, half_lanes), F32)
    zeros_bottom = jnp.zeros((HEAD_DIM - SEL_LANES - SLOPE_ROWS, half_lanes), F32)
    ones_rows = _ones_rows()
    key = lax.broadcasted_iota(jnp.int32, (MOBA_BLOCK, 2 * WINDOW), 0)
    qry = lax.broadcasted_iota(jnp.int32, (MOBA_BLOCK, 2 * WINDOW), 1) & (WINDOW - 1)
    lane_q = lax.broadcasted_iota(jnp.int32, (1, half_lanes), 1) & (WINDOW - 1)
    slopes = [slope_ref[kh, 0:1, :] + slope_ref[kh, 1:2, :] + slope_ref[kh, 2:3, :]
              for kh in range(N_KV_HEADS)]

    base, v_half, keep, sinks = {}, {}, {}, {}
    for blk in range(SWA_BLOCKS_PER_STEP):
        j = pl.program_id(1) * SWA_BLOCKS_PER_STEP + blk
        for kh in range(N_KV_HEADS):
            feat = jnp.concatenate([zeros_top, slope_ref[kh], zeros_bottom], axis=0).astype(BF16)
            for half in range(2):
                for h in range(GROUP):
                    row0 = (kh * GROUP + h) * HEAD_DIM
                    lhs_ref[blk, kh, half, 0:HEAD_DIM, h * WINDOW:(h + 1) * WINDOW] = (
                        q_ref[blk, row0:row0 + HEAD_DIM, half * WINDOW:(half + 1) * WINDOW])
                lhs_ref[blk, kh, half, HEAD_DIM:2 * HEAD_DIM, :] = feat

        base[blk] = [jnp.maximum(j * MOBA_BLOCK - WINDOW, 0), j * MOBA_BLOCK]
        v_own = vt_ref[j]
        v_prev = vt_ref[jnp.maximum(j - 1, 0)]
        v_straddle = jnp.concatenate([v_prev[:, WINDOW:], v_own[:, :WINDOW]], axis=1)
        v_half[blk] = [jnp.where(j > 0, v_straddle, v_own), v_own]
        for half in range(2):
            dist = (j * MOBA_BLOCK + half * WINDOW - base[blk][half]) + qry - key
            keep[(blk, half)] = (dist & -WINDOW) == 0
            pos = (j * MOBA_BLOCK + half * WINDOW + lane_q).astype(F32)
            for kh in range(N_KV_HEADS):
                sinks[(blk, kh, half)] = sink_ref[kh] + slopes[kh] * pos

    def keys(blk, kh, half):
        start = pl.multiple_of(base[blk][half], WINDOW)
        return k_ref[pl.ds(start, MOBA_BLOCK), kh * PAIR_W:(kh + 1) * PAIR_W]

    def values(blk, kh, half):
        return jnp.concatenate([v_half[blk][half][kh * HEAD_DIM:(kh + 1) * HEAD_DIM, :], ones_rows], axis=0)

    items = [(blk, kh, half, c) for blk in range(SWA_BLOCKS_PER_STEP) for kh in range(N_KV_HEADS)
             for c in range(PAIRS) for half in range(2)]
    raw, pvs = {}, {}

    def pair_cols(c):
        return slice(c * 2 * WINDOW, (c + 1) * 2 * WINDOW)

    def scores(i):
        blk, kh, half, c = items[i]
        raw[i] = jnp.dot(keys(blk, kh, half), lhs_ref[blk, kh, half, :, pair_cols(c)],
                         preferred_element_type=F32)

    def softmax_pv(i):
        blk, kh, half, c = items[i]
        s = jnp.where(keep[(blk, half)], raw.pop(i), NEG_INF)
        sink = sinks[(blk, kh, half)][:, pair_cols(c)]
        m = jnp.maximum(jnp.max(s, axis=0, keepdims=True), sink)
        e = jnp.exp2(s - m).astype(BF16)
        pv = jnp.dot(values(blk, kh, half), e, preferred_element_type=F32)
        pvs[i] = (pv, jnp.exp2(sink - m))

    def normalise(i):
        pv, sink_term = pvs.pop(i)
        l = pv[HEAD_DIM:HEAD_DIM + 1, :] + sink_term
        return pv[0:HEAD_DIM, :] * (1.0 / l)

    done = {}
    for i in range(len(items) + SWA_PV_LAG + 1):
        if i < len(items):
            scores(i)
        if 0 <= i - SWA_PV_LAG < len(items):
            softmax_pv(i - SWA_PV_LAG)
        n = i - SWA_PV_LAG - 1
        if 0 <= n < len(items):
            blk, kh, half, c = items[n]
            done[half] = normalise(n)
            if half == 1:
                pair = jnp.concatenate(
                    [jnp.concatenate([done[0][:, :WINDOW], done[1][:, :WINDOW]], axis=1),
                     jnp.concatenate([done[0][:, WINDOW:], done[1][:, WINDOW:]], axis=1)],
                    axis=0)
                col0 = (kh * PAIRS + c) * PAIR_W
                o_ref[blk * MOBA_BLOCK:(blk + 1) * MOBA_BLOCK, col0:col0 + PAIR_W] = (
                    pair.T.astype(o_ref.dtype))


def swa_attention(q_t, k_aug, v_t, slope_rows, sink_rows, to_bf16, *, batch, seq):
    n = k_aug.shape[0]
    nblk = seq // MOBA_BLOCK
    half_lanes = GROUP * WINDOW
    per_seq = nblk // SWA_BLOCKS_PER_STEP
    steps = batch * per_seq
    cast_specs = [pl.BlockSpec((w.shape[0], w.shape[1] // steps, w.shape[2]),
                               lambda b, j: (0, b * per_seq + j, 0)) for w in to_bf16]
    outs = pl.pallas_call(
        functools.partial(_swa_kernel, n_cast=len(to_bf16)),
        grid=(batch, per_seq),
        in_specs=[pl.BlockSpec(slope_rows.shape, lambda b, j: (0, 0, 0)),
                  pl.BlockSpec(sink_rows.shape, lambda b, j: (0, 0, 0)),
                  pl.BlockSpec((SWA_BLOCKS_PER_STEP, Q_DIM, MOBA_BLOCK), lambda b, j: (b * per_seq + j, 0, 0)),
                  pl.BlockSpec((seq, K_AUG), lambda b, j: (b, 0)),
                  pl.BlockSpec((nblk, KV_DIM, MOBA_BLOCK), lambda b, j: (b, 0, 0))] + cast_specs,
        out_specs=[pl.BlockSpec((SWA_BLOCKS_PER_STEP * MOBA_BLOCK, Q_DIM),
                                lambda b, j: (b * per_seq + j, 0))] + cast_specs,
        out_shape=[jax.ShapeDtypeStruct((n, Q_DIM), BF16)]
                  + [jax.ShapeDtypeStruct(w.shape, BF16) for w in to_bf16],
        scratch_shapes=[pltpu.VMEM((SWA_BLOCKS_PER_STEP, N_KV_HEADS, 2, 2 * HEAD_DIM, half_lanes), BF16)],
        compiler_params=_cparams(("parallel", "arbitrary")),
        name="swa_attention",
    )(slope_rows, sink_rows, q_t, k_aug, v_t, *to_bf16)
    return outs[0], outs[1:]


MOBA_PV_LAG = 2


def _moba_kernel(slope_ref, q_ref, k_ref, vt_ref, km_ref, o_ref,
                 lhs_ref, acc_ref, m_ref, s0_ref, s1_ref):
    lanes = GROUP * MOBA_BLOCK
    nblk = km_ref.shape[1]
    km = km_ref[0][:, :HEAD_DIM]
    blk = lax.broadcasted_iota(jnp.int32, (nblk, lanes), 0)
    blkf = blk.astype(F32)
    pad = jnp.zeros((HEAD_DIM - SEL_LANES - SLOPE_ROWS, lanes), F32)

    def build_operand(j, carry):
        qt = q_ref[j]
        gate = jnp.concatenate(
            [jnp.dot(km, qt[h * HEAD_DIM:(h + 1) * HEAD_DIM, :], preferred_element_type=F32)
             for h in range(GROUP)], axis=1)
        g = jnp.where(blk < j, gate, NEG_INF)
        sel = jnp.zeros((nblk, lanes), F32)
        for _ in range(MOBA_TOPK):
            mx = jnp.max(g, axis=0, keepdims=True)
            idx = jnp.min(jnp.where(g == mx, blkf, float(nblk)), axis=0, keepdims=True)
            hit = blkf == idx
            sel = jnp.where(hit, 1.0, sel)
            g = jnp.where(hit, BELOW_NEG_INF, g)
        not_sel = jnp.where(blk < j, 1.0 - sel, jnp.where(blk == j, 0.0, 1.0))
        feat = jnp.concatenate([not_sel, slope_ref[0], pad], axis=0).astype(BF16)
        for h in range(GROUP):
            lhs_ref[j, h, 0:HEAD_DIM, :] = qt[h * HEAD_DIM:(h + 1) * HEAD_DIM, :]
            lhs_ref[j, h, HEAD_DIM:2 * HEAD_DIM, :] = feat[:, h * MOBA_BLOCK:(h + 1) * MOBA_BLOCK]
        return carry

    lax.fori_loop(0, nblk, build_operand, 0, unroll=4)
    m_ref[...] = jnp.full(m_ref.shape, RUNNING_MAX_INIT, F32)
    acc_ref[...] = jnp.zeros(acc_ref.shape, F32)

    ones_rows = _ones_rows()
    key = lax.broadcasted_iota(jnp.int32, (MOBA_BLOCK, MOBA_BLOCK), 0)
    qry = lax.broadcasted_iota(jnp.int32, (MOBA_BLOCK, MOBA_BLOCK), 1)

    def key_block(n):
        return k_ref[pl.ds(pl.multiple_of(n * MOBA_BLOCK, MOBA_BLOCK), MOBA_BLOCK), :]

    def values(n):
        return jnp.concatenate([vt_ref[n], ones_rows], axis=0)

    def softmax(src_ref, j, h):
        s = src_ref[h]
        m_old = m_ref[j, h]
        m_new = jnp.maximum(m_old, jnp.max(s, axis=0, keepdims=True))
        m_ref[j, h] = m_new
        return j, h, jnp.exp2(s - m_new).astype(BF16), jnp.exp2(m_old - m_new)

    def accumulate(v_aug, j, h, e, alpha):
        pv = jnp.dot(v_aug, e, preferred_element_type=F32)
        acc_ref[j, h] = acc_ref[j, h] * alpha + pv

    def step(j, t, src_ref, dst_ref, next_own):
        if next_own:
            jn = jnp.minimum(j + 1, nblk - 1)
            kblk = key_block(jn)
        else:
            jn = j
            kblk = key_block(t)
        v_aug = values(jnp.where(t == 0, j, t - 1))
        pending = []
        for h in range(GROUP):
            s_next = jnp.dot(kblk, lhs_ref[jn, h], preferred_element_type=F32)
            dst_ref[h] = jnp.where(key <= qry, s_next, NEG_INF) if next_own else s_next
            pending.append(softmax(src_ref, j, h))
            if len(pending) > MOBA_PV_LAG:
                accumulate(v_aug, *pending.pop(0))
        for ready in pending:
            accumulate(v_aug, *ready)

    def query_block(j, cur_ref, other_ref, odd):
        def two_steps(i, carry):
            step(j, 2 * i, cur_ref, other_ref, False)
            step(j, 2 * i + 1, other_ref, cur_ref, False)
            return carry

        lax.fori_loop(0, j // 2, two_steps, 0)
        if odd:
            step(j, j - 1, cur_ref, other_ref, False)
            step(j, j, other_ref, cur_ref, True)
        else:
            step(j, j, cur_ref, other_ref, True)

    k_first = key_block(0)
    for h in range(GROUP):
        s0_ref[h] = jnp.where(key <= qry, jnp.dot(k_first, lhs_ref[0, h], preferred_element_type=F32),
                              NEG_INF)

    def four_query_blocks(i, carry):
        query_block(4 * i, s0_ref, s1_ref, False)
        query_block(4 * i + 1, s1_ref, s0_ref, True)
        query_block(4 * i + 2, s1_ref, s0_ref, False)
        query_block(4 * i + 3, s0_ref, s1_ref, True)
        return carry

    lax.fori_loop(0, nblk // 4, four_query_blocks, 0)

    def write_out(j, carry):
        rows = pl.ds(pl.multiple_of(j * MOBA_BLOCK, MOBA_BLOCK), MOBA_BLOCK)
        for p in range(PAIRS):
            pair = jnp.concatenate(
                [acc_ref[j, h, 0:HEAD_DIM, :] * (1.0 / acc_ref[j, h, HEAD_DIM:HEAD_DIM + 1, :])
                 for h in (2 * p, 2 * p + 1)], axis=0)
            o_ref[rows, p * PAIR_W:(p + 1) * PAIR_W] = pair.T.astype(o_ref.dtype)
        return carry

    lax.fori_loop(0, nblk, write_out, 0, unroll=2)


def moba_attention(q_t, k_aug, v_t, kmean, slope_rows, *, batch, seq):
    n = k_aug.shape[0]
    nblk = seq // MOBA_BLOCK
    assert nblk % 4 == 0
    lanes = GROUP * MOBA_BLOCK
    return pl.pallas_call(
        _moba_kernel,
        grid=(batch, N_KV_HEADS),
        in_specs=[pl.BlockSpec((1, SLOPE_ROWS, lanes), lambda b, kh: (kh, 0, 0)),
                  pl.BlockSpec((nblk, GROUP * HEAD_DIM, MOBA_BLOCK), lambda b, kh: (b, kh, 0)),
                  pl.BlockSpec((seq, PAIR_W), lambda b, kh: (b, kh)),
                  pl.BlockSpec((nblk, HEAD_DIM, MOBA_BLOCK), lambda b, kh: (b, kh, 0)),
                  pl.BlockSpec((1, nblk, PAIR_W), lambda b, kh: (b, 0, kh))],
        out_specs=pl.BlockSpec((seq, GROUP * HEAD_DIM), lambda b, kh: (b, kh)),
        out_shape=jax.ShapeDtypeStruct((n, Q_DIM), BF16),
        scratch_shapes=[pltpu.VMEM((nblk, GROUP, 2 * HEAD_DIM, MOBA_BLOCK), BF16),
                        pltpu.VMEM((nblk, GROUP, HEAD_DIM + ONES_ROWS, MOBA_BLOCK), F32),
                        pltpu.VMEM((nblk, GROUP, 1, MOBA_BLOCK), F32),
                        pltpu.VMEM((GROUP, MOBA_BLOCK, MOBA_BLOCK), F32),
                        pltpu.VMEM((GROUP, MOBA_BLOCK, MOBA_BLOCK), F32)],
        compiler_params=_cparams(("parallel", "arbitrary")),
        name="moba_attention",
    )(slope_rows, q_t, k_aug, v_t, kmean)


def _alibi_slopes_base2():
    slopes = np.exp2(-8.0 * np.arange(1, N_HEADS + 1, dtype=np.float32) / N_HEADS)
    return (slopes.astype(np.float32) * np.float32(LOG2E)).astype(np.float32)


def _slope_rows(slopes, width):
    hi = slopes.astype(BF16).astype(np.float32)
    mid = (slopes - hi).astype(BF16).astype(np.float32)
    lo = (slopes - hi - mid).astype(BF16).astype(np.float32)
    pieces = np.stack([hi, mid, lo], axis=-1)
    pad = np.zeros((N_HEADS, SLOPE_ROWS - 2 * SLOPE_PIECES), np.float32)
    rows = np.concatenate([pieces, pieces, pad], axis=1)
    rows = rows.reshape(N_KV_HEADS, GROUP, SLOPE_ROWS).transpose(0, 2, 1)
    return np.repeat(rows, width, axis=2)


def _key_features(seq):
    pos = np.arange(seq, dtype=np.int32)[:, None]
    blk = pos // MOBA_BLOCK
    lane = np.arange(PAIR_W, dtype=np.int32)[None, :] - HEAD_DIM
    feat = np.where(lane == blk, np.float32(NEG_INF), np.float32(0.0))
    feat = np.where((lane >= SLOPE_BLK_LANE) & (lane < SLOPE_POS_LANE),
                    (blk * MOBA_BLOCK).astype(np.float32), feat)
    feat = np.where((lane >= SLOPE_POS_LANE) & (lane < SLOPE_POS_LANE + SLOPE_PIECES),
                    (pos % MOBA_BLOCK).astype(np.float32), feat)
    return np.where(lane < 0, np.float32(0.0), feat).astype(np.float32)


def kernel(x, w_qkv_a, sinks_a, w_o_a, kv_norm, w_kv_shared, w_q_b, w_o_b,
           norm_attn_pre, norm_attn_post, norm_mlp_pre, norm_mlp_post, w_up, w_down):
    batch, seq, d = x.shape
    n = batch * seq
    assert seq // MOBA_BLOCK == SEL_LANES and seq % MOBA_BLOCK == 0
    qscale = HEAD_DIM ** -0.5 * LOG2E
    slopes2 = _alibi_slopes_base2()
    key_feat = _key_features(seq)

    wq_a, wk_a, wv_a = jnp.split(w_qkv_a[0], [Q_DIM, Q_DIM + KV_DIM], axis=-1)
    wk_s, wv_s = jnp.split(w_kv_shared, [KV_DIM], axis=-1)

    xf = x.reshape(n, d)

    q0_t, k0, v0_t = attn_proj(
        xf, norm_attn_pre[0], norm_attn_pre[0], (wq_a * qscale).astype(BF16).T,
        wk_a.astype(BF16), wv_a.astype(BF16).T, key_feat,
        tm=512, seq=seq, shared_norm=True, with_kmean=False, name="attn_proj0")
    sink_rows = jnp.repeat((sinks_a[0].astype(F32) * LOG2E).reshape(N_KV_HEADS, 1, GROUP), WINDOW, axis=2)
    mix0, (wup_all, wdn_all, wo_a, wo_b) = swa_attention(
        q0_t, k0, v0_t, _slope_rows(slopes2, WINDOW), sink_rows, [w_up, w_down, w_o_a, w_o_b],
        batch=batch, seq=seq)
    x1 = oproj_norm_residual(mix0, xf, norm_attn_post[0], wo_a[0], tm=512, name="oproj0")
    x2 = mlp_block(x1, norm_mlp_pre[0], norm_mlp_post[0], wup_all, wdn_all, 0, tm=512, tf=1024, name="mlp0")

    q1_t, k1, v1_t, kmean = attn_proj(
        x2, norm_attn_pre[1], kv_norm, (w_q_b[0] * qscale).astype(BF16).T,
        wk_s.astype(BF16), wv_s.astype(BF16).T, key_feat,
        tm=512, seq=seq, shared_norm=False, with_kmean=True, name="attn_proj1")
    kmean = kmean.reshape(batch, seq // MOBA_BLOCK, K_AUG).astype(BF16)
    mix1 = moba_attention(q1_t, k1, v1_t, kmean, _slope_rows(slopes2, MOBA_BLOCK), batch=batch, seq=seq)
    x3 = oproj_norm_residual(mix1, x2, norm_attn_post[1], wo_b[0], tm=512, name="oproj1")
    x4 = mlp_block(x3, norm_mlp_pre[1], norm_mlp_post[1], wup_all, wdn_all, 1, tm=512, tf=1024, name="mlp1")
    return x4.reshape(batch, seq, d)
```

```python
import functools

import jax
import jax.numpy as jnp
import numpy as np
from jax import lax
from jax.experimental import pallas as pl
from jax.experimental.pallas import tpu as pltpu

D_MODEL = 2048
N_HEADS = 32
N_KV_HEADS = 4
HEAD_DIM = 64
GROUP = N_HEADS // N_KV_HEADS
PAIRS = GROUP // 2
PAIR_W = 2 * HEAD_DIM
Q_DIM = N_HEADS * HEAD_DIM
KV_DIM = N_KV_HEADS * HEAD_DIM
K_AUG = N_KV_HEADS * PAIR_W
D_FF = 4 * D_MODEL
WINDOW = 128
MOBA_BLOCK = 256
MOBA_TOPK = 3
RMS_EPS = 1e-6
NEG_INF = -1e30
BELOW_NEG_INF = -3e38
RUNNING_MAX_INIT = -1e38
LOG2E = 1.4426950408889634

SEL_LANES = 16
SLOPE_BLK_LANE = 16
SLOPE_POS_LANE = 19
SLOPE_PIECES = 3
SLOPE_ROWS = 8
ONES_ROWS = 16
NORM_ROW_CHUNK = 8

VMEM_LIMIT = 56 * 1024 * 1024

BF16 = jnp.bfloat16
F32 = jnp.float32
NT_DIMS = (((1,), (1,)), ((), ()))


def _rms_scale(x):
    return lax.rsqrt(jnp.mean(x * x, axis=-1, keepdims=True) + RMS_EPS)


def _cparams(semantics):
    return pltpu.CompilerParams(dimension_semantics=semantics,
                                vmem_limit_bytes=VMEM_LIMIT)


def _ones_rows():
    row = lax.broadcasted_iota(jnp.int32, (ONES_ROWS, MOBA_BLOCK), 0)
    return jnp.where(row == 0, 1.0, 0.0).astype(BF16)


PROJ_CHUNK = 512


def _attn_proj_kernel(x_ref, gq_ref, gkv_ref, wqt_ref, wk_ref, wvt_ref, feat_ref,
                      qt_ref, k_ref, vt_ref, *km_refs, shared_norm):
    x = x_ref[...]
    xn = x * _rms_scale(x)
    hq = (xn * gq_ref[...]).astype(BF16)
    hkv = hq if shared_norm else (xn * gkv_ref[...]).astype(BF16)
    nblk = x_ref.shape[0] // MOBA_BLOCK
    for c in range(0, wqt_ref.shape[0], PROJ_CHUNK):
        qt = lax.dot_general(wqt_ref[c:c + PROJ_CHUNK, :], hq, NT_DIMS,
                             preferred_element_type=F32).astype(BF16)
        for blk in range(nblk):
            qt_ref[blk, c:c + PROJ_CHUNK, :] = qt[:, blk * MOBA_BLOCK:(blk + 1) * MOBA_BLOCK]
    kacc = jnp.dot(hkv, wk_ref[...], preferred_element_type=F32)
    feat = feat_ref[...]
    lane = lax.broadcasted_iota(jnp.int32, feat.shape, 1)
    for kh in range(N_KV_HEADS):
        two_heads = kacc[:, (kh // 2) * PAIR_W:(kh // 2 + 1) * PAIR_W]
        if kh % 2:
            two_heads = jnp.concatenate([two_heads[:, HEAD_DIM:], two_heads[:, :HEAD_DIM]], axis=1)
        k_head = jnp.where(lane < HEAD_DIM, two_heads, 0.0)
        cols = slice(kh * PAIR_W, (kh + 1) * PAIR_W)
        for km_ref in km_refs:
            for blk in range(nblk):
                rows = k_head[blk * MOBA_BLOCK:(blk + 1) * MOBA_BLOCK]
                km_ref[blk, :, cols] = jnp.sum(rows, axis=0, keepdims=True) * (1.0 / MOBA_BLOCK)
        k_ref[:, cols] = (k_head + feat).astype(BF16)
    vt = lax.dot_general(wvt_ref[...], hkv, NT_DIMS, preferred_element_type=F32)
    for blk in range(nblk):
        vt_ref[blk] = vt[:, blk * MOBA_BLOCK:(blk + 1) * MOBA_BLOCK].astype(BF16)


def attn_proj(x, gq, gkv, wq_t, wk, wv_t, key_feat, *, tm, seq, shared_norm, with_kmean, name):
    n, d = x.shape
    nb = tm // MOBA_BLOCK
    tiles_per_seq = seq // tm
    whole = lambda w: pl.BlockSpec(w.shape, lambda i: (0, 0))
    gain = pl.BlockSpec((1, d), lambda i: (0, 0))
    out_specs = [pl.BlockSpec((nb, wq_t.shape[0], MOBA_BLOCK), lambda i: (i, 0, 0)),
                 pl.BlockSpec((tm, K_AUG), lambda i: (i, 0)),
                 pl.BlockSpec((nb, KV_DIM, MOBA_BLOCK), lambda i: (i, 0, 0))]
    out_shape = [jax.ShapeDtypeStruct((n // MOBA_BLOCK, wq_t.shape[0], MOBA_BLOCK), BF16),
                 jax.ShapeDtypeStruct((n, K_AUG), BF16),
                 jax.ShapeDtypeStruct((n // MOBA_BLOCK, KV_DIM, MOBA_BLOCK), BF16)]
    if with_kmean:
        out_specs.append(pl.BlockSpec((nb, 1, K_AUG), lambda i: (i, 0, 0)))
        out_shape.append(jax.ShapeDtypeStruct((n // MOBA_BLOCK, 1, K_AUG), F32))
    return pl.pallas_call(
        functools.partial(_attn_proj_kernel, shared_norm=shared_norm),
        grid=(n // tm,),
        in_specs=[pl.BlockSpec((tm, d), lambda i: (i, 0)), gain, gain,
                  whole(wq_t), whole(wk), whole(wv_t),
                  pl.BlockSpec((tm, PAIR_W), lambda i: (i % tiles_per_seq, 0))],
        out_specs=out_specs,
        out_shape=out_shape,
        compiler_params=_cparams(("parallel",)),
        name=name,
    )(x, gq.reshape(1, d), gkv.reshape(1, d), wq_t, wk, wv_t, key_feat)


def _oproj_kernel(mix_ref, x_ref, g_ref, w_ref, o_ref):
    a = jnp.dot(mix_ref[...], w_ref[...], preferred_element_type=F32)
    o_ref[...] = x_ref[...] + (a * _rms_scale(a)) * g_ref[...]


def oproj_norm_residual(mix, x, g, w, *, tm, name):
    n, d = x.shape
    return pl.pallas_call(
        _oproj_kernel,
        grid=(n // tm,),
        in_specs=[pl.BlockSpec((tm, mix.shape[1]), lambda i: (i, 0)),
                  pl.BlockSpec((tm, d), lambda i: (i, 0)),
                  pl.BlockSpec((1, d), lambda i: (0, 0)),
                  pl.BlockSpec(w.shape, lambda i: (0, 0))],
        out_specs=pl.BlockSpec((tm, d), lambda i: (i, 0)),
        out_shape=jax.ShapeDtypeStruct((n, d), F32),
        compiler_params=_cparams(("parallel",)),
        name=name,
    )(mix, x, g.reshape(1, d), w)


MLP_FF_CHUNK = 512


def _mlp_kernel(x_ref, gpre_ref, gpost_ref, wup_ref, wdn_ref, o_ref, h_ref, acc_ref):
    f = pl.program_id(1)

    @pl.when(f == 0)
    def _():
        x = x_ref[...]
        h_ref[...] = ((x * _rms_scale(x)) * gpre_ref[...]).astype(BF16)
        acc_ref[...] = jnp.zeros(acc_ref.shape, F32)

    h = h_ref[...]
    chunks = wup_ref.shape[1] // MLP_FF_CHUNK

    def up(c):
        cols = slice(c * MLP_FF_CHUNK, (c + 1) * MLP_FF_CHUNK)
        return jnp.dot(h, wup_ref[:, cols], preferred_element_type=F32)

    u_next = up(0)
    for c in range(chunks):
        u = jnp.maximum(u_next, 0.0)
        if c + 1 < chunks:
            u_next = up(c + 1)
        rows = slice(c * MLP_FF_CHUNK, (c + 1) * MLP_FF_CHUNK)
        acc_ref[...] += jnp.dot((u * u).astype(BF16), wdn_ref[rows, :], preferred_element_type=F32)

    @pl.when(f == pl.num_programs(1) - 1)
    def _():
        for r in range(0, acc_ref.shape[0], NORM_ROW_CHUNK):
            a = acc_ref[r:r + NORM_ROW_CHUNK, :]
            o_ref[r:r + NORM_ROW_CHUNK, :] = (
                x_ref[r:r + NORM_ROW_CHUNK, :] + (a * _rms_scale(a)) * gpost_ref[...])


def mlp_block(x, gpre, gpost, wup_all, wdn_all, layer, *, tm, tf, name):
    n, d = x.shape
    dff = wup_all.shape[2]
    return pl.pallas_call(
        _mlp_kernel,
        grid=(n // tm, dff // tf),
        in_specs=[pl.BlockSpec((tm, d), lambda i, f: (i, 0)),
                  pl.BlockSpec((1, d), lambda i, f: (0, 0)),
                  pl.BlockSpec((1, d), lambda i, f: (0, 0)),
                  pl.BlockSpec((None, d, tf), lambda i, f: (layer, 0, f)),
                  pl.BlockSpec((None, tf, d), lambda i, f: (layer, f, 0))],
        out_specs=pl.BlockSpec((tm, d), lambda i, f: (i, 0)),
        out_shape=jax.ShapeDtypeStruct((n, d), F32),
        scratch_shapes=[pltpu.VMEM((tm, d), BF16), pltpu.VMEM((tm, d), F32)],
        compiler_params=_cparams(("parallel", "arbitrary")),
        name=name,
    )(x, gpre.reshape(1, d), gpost.reshape(1, d), wup_all, wdn_all)


SWA_PV_LAG = 6
SWA_BLOCKS_PER_STEP = 2


def _swa_kernel(slope_ref, sink_ref, q_ref, k_ref, vt_ref, *refs, n_cast):
    o_ref, lhs_ref = refs[n_cast], refs[-1]
    for src_ref, dst_ref in zip(refs[:n_cast], refs[n_cast + 1:2 * n_cast + 1]):
        dst_ref[...] = src_ref[...].astype(BF16)

    half_lanes = GROUP * WINDOW
    zeros_top = jnp.zeros((SEL_LANES, half_lanes), F32)
    zeros_bottom = jnp.zeros((HEAD_DIM - SEL_LANES - SLOPE_ROWS, half_lanes), F32)
    ones_rows = _ones_rows()
    key = lax.broadcasted_iota(jnp.int32, (MOBA_BLOCK, 2 * WINDOW), 0)
    qry = lax.broadcasted_iota(jnp.int32, (MOBA_BLOCK, 2 * WINDOW), 1) & (WINDOW - 1)
    lane_q = lax.broadcasted_iota(jnp.int32, (1, half_lanes), 1) & (WINDOW - 1)
    slopes = [slope_ref[kh, 0:1, :] + slope_ref[kh, 1:2, :] + slope_ref[kh, 2:3, :]
              for kh in range(N_KV_HEADS)]

    base, v_half, keep, sinks = {}, {}, {}, {}
    for blk in range(SWA_BLOCKS_PER_STEP):
        j = pl.program_id(1) * SWA_BLOCKS_PER_STEP + blk
        for kh in range(N_KV_HEADS):
            feat = jnp.concatenate([zeros_top, slope_ref[kh], zeros_bottom], axis=0).astype(BF16)
            for half in range(2):
                for h in range(GROUP):
                    row0 = (kh * GROUP + h) * HEAD_DIM
                    lhs_ref[blk, kh, half, 0:HEAD_DIM, h * WINDOW:(h + 1) * WINDOW] = (
                        q_ref[blk, row0:row0 + HEAD_DIM, half * WINDOW:(half + 1) * WINDOW])
                lhs_ref[blk, kh, half, HEAD_DIM:2 * HEAD_DIM, :] = feat

        base[blk] = [jnp.maximum(j * MOBA_BLOCK - WINDOW, 0), j * MOBA_BLOCK]
        v_own = vt_ref[j]
        v_prev = vt_ref[jnp.maximum(j - 1, 0)]
        v_straddle = jnp.concatenate([v_prev[:, WINDOW:], v_own[:, :WINDOW]], axis=1)
        v_half[blk] = [jnp.where(j > 0, v_straddle, v_own), v_own]
        for half in range(2):
            dist = (j * MOBA_BLOCK + half * WINDOW - base[blk][half]) + qry - key
            keep[(blk, half)] = (dist & -WINDOW) == 0
            pos = (j * MOBA_BLOCK + half * WINDOW + lane_q).astype(F32)
            for kh in range(N_KV_HEADS):
                sinks[(blk, kh, half)] = sink_ref[kh] + slopes[kh] * pos

    def keys(blk, kh, half):
        start = pl.multiple_of(base[blk][half], WINDOW)
        return k_ref[pl.ds(start, MOBA_BLOCK), kh * PAIR_W:(kh + 1) * PAIR_W]

    def values(blk, kh, half):
        return jnp.concatenate([v_half[blk][half][kh * HEAD_DIM:(kh + 1) * HEAD_DIM, :], ones_rows], axis=0)

    items = [(blk, kh, half, c) for blk in range(SWA_BLOCKS_PER_STEP) for kh in range(N_KV_HEADS)
             for c in range(PAIRS) for half in range(2)]
    raw, pvs = {}, {}

    def pair_cols(c):
        return slice(c * 2 * WINDOW, (c + 1) * 2 * WINDOW)

    def scores(i):
        blk, kh, half, c = items[i]
        raw[i] = jnp.dot(keys(blk, kh, half), lhs_ref[blk, kh, half, :, pair_cols(c)],
                         preferred_element_type=F32)

    def softmax_pv(i):
        blk, kh, half, c = items[i]
        s = jnp.where(keep[(blk, half)], raw.pop(i), NEG_INF)
        sink = sinks[(blk, kh, half)][:, pair_cols(c)]
        m = jnp.maximum(jnp.max(s, axis=0, keepdims=True), sink)
        e = jnp.exp2(s - m).astype(BF16)
        pv = jnp.dot(values(blk, kh, half), e, preferred_element_type=F32)
        pvs[i] = (pv, jnp.exp2(sink - m))

    def normalise(i):
        pv, sink_term = pvs.pop(i)
        l = pv[HEAD_DIM:HEAD_DIM + 1, :] + sink_term
        return pv[0:HEAD_DIM, :] * (1.0 / l)

    done = {}
    for i in range(len(items) + SWA_PV_LAG + 1):
        if i < len(items):
            scores(i)
        if 0 <= i - SWA_PV_LAG < len(items):
            softmax_pv(i - SWA_PV_LAG)
        n = i - SWA_PV_LAG - 1
        if 0 <= n < len(items):
            blk, kh, half, c = items[n]
            done[half] = normalise(n)
            if half == 1:
                pair = jnp.concatenate(
                    [jnp.concatenate([done[0][:, :WINDOW], done[1][:, :WINDOW]], axis=1),
                     jnp.concatenate([done[0][:, WINDOW:], done[1][:, WINDOW:]], axis=1)],
                    axis=0)
                col0 = (kh * PAIRS + c) * PAIR_W
                o_ref[blk * MOBA_BLOCK:(blk + 1) * MOBA_BLOCK, col0:col0 + PAIR_W] = (
                    pair.T.astype(o_ref.dtype))


def swa_attention(q_t, k_aug, v_t, slope_rows, sink_rows, to_bf16, *, batch, seq):
    n = k_aug.shape[0]
    nblk = seq // MOBA_BLOCK
    half_lanes = GROUP * WINDOW
    per_seq = nblk // SWA_BLOCKS_PER_STEP
    steps = batch * per_seq
    cast_specs = [pl.BlockSpec((w.shape[0], w.shape[1] // steps, w.shape[2]),
                               lambda b, j: (0, b * per_seq + j, 0)) for w in to_bf16]
    outs = pl.pallas_call(
        functools.partial(_swa_kernel, n_cast=len(to_bf16)),
        grid=(batch, per_seq),
        in_specs=[pl.BlockSpec(slope_rows.shape, lambda b, j: (0, 0, 0)),
                  pl.BlockSpec(sink_rows.shape, lambda b, j: (0, 0, 0)),
                  pl.BlockSpec((SWA_BLOCKS_PER_STEP, Q_DIM, MOBA_BLOCK), lambda b, j: (b * per_seq + j, 0, 0)),
                  pl.BlockSpec((seq, K_AUG), lambda b, j: (b, 0)),
                  pl.BlockSpec((nblk, KV_DIM, MOBA_BLOCK), lambda b, j: (b, 0, 0))] + cast_specs,
        out_specs=[pl.BlockSpec((SWA_BLOCKS_PER_STEP * MOBA_BLOCK, Q_DIM),
                                lambda b, j: (b * per_seq + j, 0))] + cast_specs,
        out_shape=[jax.ShapeDtypeStruct((n, Q_DIM), BF16)]
                  + [jax.ShapeDtypeStruct(w.shape, BF16) for w in to_bf16],
        scratch_shapes=[pltpu.VMEM((SWA_BLOCKS_PER_STEP, N_KV_HEADS, 2, 2 * HEAD_DIM, half_lanes), BF16)],
        compiler_params=_cparams(("parallel", "arbitrary")),
        name="swa_attention",
    )(slope_rows, sink_rows, q_t, k_aug, v_t, *to_bf16)
    return outs[0], outs[1:]


MOBA_PV_LAG = 2


def _moba_kernel(slope_ref, q_ref, k_ref, vt_ref, km_ref, o_ref,
                 lhs_ref, acc_ref, m_ref, s0_ref, s1_ref):
    lanes = GROUP * MOBA_BLOCK
    nblk = km_ref.shape[1]
    km = km_ref[0][:, :HEAD_DIM]
    blk = lax.broadcasted_iota(jnp.int32, (nblk, lanes), 0)
    blkf = blk.astype(F32)
    pad = jnp.zeros((HEAD_DIM - SEL_LANES - SLOPE_ROWS, lanes), F32)

    def build_operand(j, carry):
        qt = q_ref[j]
        gate = jnp.concatenate(
            [jnp.dot(km, qt[h * HEAD_DIM:(h + 1) * HEAD_DIM, :], preferred_element_type=F32)
             for h in range(GROUP)], axis=1)
        g = jnp.where(blk < j, gate, NEG_INF)
        sel = jnp.zeros((nblk, lanes), F32)
        for _ in range(MOBA_TOPK):
            mx = jnp.max(g, axis=0, keepdims=True)
            idx = jnp.min(jnp.where(g == mx, blkf, float(nblk)), axis=0, keepdims=True)
            hit = blkf == idx
            sel = jnp.where(hit, 1.0, sel)
            g = jnp.where(hit, BELOW_NEG_INF, g)
        not_sel = jnp.where(blk < j, 1.0 - sel, jnp.where(blk == j, 0.0, 1.0))
        feat = jnp.concatenate([not_sel, slope_ref[0], pad], axis=0).astype(BF16)
        for h in range(GROUP):
            lhs_ref[j, h, 0:HEAD_DIM, :] = qt[h * HEAD_DIM:(h + 1) * HEAD_DIM, :]
            lhs_ref[j, h, HEAD_DIM:2 * HEAD_DIM, :] = feat[:, h * MOBA_BLOCK:(h + 1) * MOBA_BLOCK]
        return carry

    lax.fori_loop(0, nblk, build_operand, 0, unroll=4)
    m_ref[...] = jnp.full(m_ref.shape, RUNNING_MAX_INIT, F32)
    acc_ref[...] = jnp.zeros(acc_ref.shape, F32)

    ones_rows = _ones_rows()
    key = lax.broadcasted_iota(jnp.int32, (MOBA_BLOCK, MOBA_BLOCK), 0)
    qry = lax.broadcasted_iota(jnp.int32, (MOBA_BLOCK, MOBA_BLOCK), 1)

    def key_block(n):
        return k_ref[pl.ds(pl.multiple_of(n * MOBA_BLOCK, MOBA_BLOCK), MOBA_BLOCK), :]

    def values(n):
        return jnp.concatenate([vt_ref[n], ones_rows], axis=0)

    def softmax(src_ref, j, h):
        s = src_ref[h]
        m_old = m_ref[j, h]
        m_new = jnp.maximum(m_old, jnp.max(s, axis=0, keepdims=True))
        m_ref[j, h] = m_new
        return j, h, jnp.exp2(s - m_new).astype(BF16), jnp.exp2(m_old - m_new)

    def accumulate(v_aug, j, h, e, alpha):
        pv = jnp.dot(v_aug, e, preferred_element_type=F32)
        acc_ref[j, h] = acc_ref[j, h] * alpha + pv

    def step(j, t, src_ref, dst_ref, next_own):
        if next_own:
            jn = jnp.minimum(j + 1, nblk - 1)
            kblk = key_block(jn)
        else:
            jn = j
            kblk = key_block(t)
        v_aug = values(jnp.where(t == 0, j, t - 1))
        pending = []
        for h in range(GROUP):
            s_next = jnp.dot(kblk, lhs_ref[jn, h], preferred_element_type=F32)
            dst_ref[h] = jnp.where(key <= qry, s_next, NEG_INF) if next_own else s_next
            pending.append(softmax(src_ref, j, h))
            if len(pending) > MOBA_PV_LAG:
                accumulate(v_aug, *pending.pop(0))
        for ready in pending:
            accumulate(v_aug, *ready)

    def query_block(j, cur_ref, other_ref, odd):
        def two_steps(i, carry):
            step(j, 2 * i, cur_ref, other_ref, False)
            step(j, 2 * i + 1, other_ref, cur_ref, False)
            return carry

        def four_steps(i, carry):
            two_steps(2 * i, carry)
            return two_steps(2 * i + 1, carry)

        pairs = j // 2
        lax.fori_loop(0, pairs // 2, four_steps, 0)
        lax.fori_loop(2 * (pairs // 2), pairs, two_steps, 0)
        if odd:
            step(j, j - 1, cur_ref, other_ref, False)
            step(j, j, other_ref, cur_ref, True)
        else:
            step(j, j, cur_ref, other_ref, True)

    k_first = key_block(0)
    for h in range(GROUP):
        s0_ref[h] = jnp.where(key <= qry, jnp.dot(k_first, lhs_ref[0, h], preferred_element_type=F32),
                              NEG_INF)

    def four_query_blocks(i, carry):
        query_block(4 * i, s0_ref, s1_ref, False)
        query_block(4 * i + 1, s1_ref, s0_ref, True)
        query_block(4 * i + 2, s1_ref, s0_ref, False)
        query_block(4 * i + 3, s0_ref, s1_ref, True)
        return carry

    lax.fori_loop(0, nblk // 4, four_query_blocks, 0)

    def write_out(j, carry):
        rows = pl.ds(pl.multiple_of(j * MOBA_BLOCK, MOBA_BLOCK), MOBA_BLOCK)
        for p in range(PAIRS):
            pair = jnp.concatenate(
                [acc_ref[j, h, 0:HEAD_DIM, :] * (1.0 / acc_ref[j, h, HEAD_DIM:HEAD_DIM + 1, :])
                 for h in (2 * p, 2 * p + 1)], axis=0)
            o_ref[rows, p * PAIR_W:(p + 1) * PAIR_W] = pair.T.astype(o_ref.dtype)
        return carry

    lax.fori_loop(0, nblk, write_out, 0, unroll=2)


def moba_attention(q_t, k_aug, v_t, kmean, slope_rows, *, batch, seq):
    n = k_aug.shape[0]
    nblk = seq // MOBA_BLOCK
    assert nblk % 4 == 0
    lanes = GROUP * MOBA_BLOCK
    return pl.pallas_call(
        _moba_kernel,
        grid=(batch, N_KV_HEADS),
        in_specs=[pl.BlockSpec((1, SLOPE_ROWS, lanes), lambda b, kh: (kh, 0, 0)),
                  pl.BlockSpec((nblk, GROUP * HEAD_DIM, MOBA_BLOCK), lambda b, kh: (b, kh, 0)),
                  pl.BlockSpec((seq, PAIR_W), lambda b, kh: (b, kh)),
                  pl.BlockSpec((nblk, HEAD_DIM, MOBA_BLOCK), lambda b, kh: (b, kh, 0)),
                  pl.BlockSpec((1, nblk, PAIR_W), lambda b, kh: (b, 0, kh))],
        out_specs=pl.BlockSpec((seq, GROUP * HEAD_DIM), lambda b, kh: (b, kh)),
        out_shape=jax.ShapeDtypeStruct((n, Q_DIM), BF16),
        scratch_shapes=[pltpu.VMEM((nblk, GROUP, 2 * HEAD_DIM, MOBA_BLOCK), BF16),
                        pltpu.VMEM((nblk, GROUP, HEAD_DIM + ONES_ROWS, MOBA_BLOCK), F32),
                        pltpu.VMEM((nblk, GROUP, 1, MOBA_BLOCK), F32),
                        pltpu.VMEM((GROUP, MOBA_BLOCK, MOBA_BLOCK), F32),
                        pltpu.VMEM((GROUP, MOBA_BLOCK, MOBA_BLOCK), F32)],
        compiler_params=_cparams(("parallel", "arbitrary")),
        name="moba_attention",
    )(slope_rows, q_t, k_aug, v_t, kmean)


def _alibi_slopes_base2():
    slopes = np.exp2(-8.0 * np.arange(1, N_HEADS + 1, dtype=np.float32) / N_HEADS)
    return (slopes.astype(np.float32) * np.float32(LOG2E)).astype(np.float32)


def _slope_rows(slopes, width):
    hi = slopes.astype(BF16).astype(np.float32)
    mid = (slopes - hi).astype(BF16).astype(np.float32)
    lo = (slopes - hi - mid).astype(BF16).astype(np.float32)
    pieces = np.stack([hi, mid, lo], axis=-1)
    pad = np.zeros((N_HEADS, SLOPE_ROWS - 2 * SLOPE_PIECES), np.float32)
    rows = np.concatenate([pieces, pieces, pad], axis=1)
    rows = rows.reshape(N_KV_HEADS, GROUP, SLOPE_ROWS).transpose(0, 2, 1)
    return np.repeat(rows, width, axis=2)


def _key_features(seq):
    pos = np.arange(seq, dtype=np.int32)[:, None]
    blk = pos // MOBA_BLOCK
    lane = np.arange(PAIR_W, dtype=np.int32)[None, :] - HEAD_DIM
    feat = np.where(lane == blk, np.float32(NEG_INF), np.float32(0.0))
    feat = np.where((lane >= SLOPE_BLK_LANE) & (lane < SLOPE_POS_LANE),
                    (blk * MOBA_BLOCK).astype(np.float32), feat)
    feat = np.where((lane >= SLOPE_POS_LANE) & (lane < SLOPE_POS_LANE + SLOPE_PIECES),
                    (pos % MOBA_BLOCK).astype(np.float32), feat)
    return np.where(lane < 0, np.float32(0.0), feat).astype(np.float32)


def kernel(x, w_qkv_a, sinks_a, w_o_a, kv_norm, w_kv_shared, w_q_b, w_o_b,
           norm_attn_pre, norm_attn_post, norm_mlp_pre, norm_mlp_post, w_up, w_down):
    batch, seq, d = x.shape
    n = batch * seq
    assert seq // MOBA_BLOCK == SEL_LANES and seq % MOBA_BLOCK == 0
    qscale = HEAD_DIM ** -0.5 * LOG2E
    slopes2 = _alibi_slopes_base2()
    key_feat = _key_features(seq)

    wq_a, wk_a, wv_a = jnp.split(w_qkv_a[0], [Q_DIM, Q_DIM + KV_DIM], axis=-1)
    wk_s, wv_s = jnp.split(w_kv_shared, [KV_DIM], axis=-1)

    xf = x.reshape(n, d)

    q0_t, k0, v0_t = attn_proj(
        xf, norm_attn_pre[0], norm_attn_pre[0], (wq_a * qscale).astype(BF16).T,
        wk_a.astype(BF16), wv_a.astype(BF16).T, key_feat,
        tm=512, seq=seq, shared_norm=True, with_kmean=False, name="attn_proj0")
    sink_rows = jnp.repeat((sinks_a[0].astype(F32) * LOG2E).reshape(N_KV_HEADS, 1, GROUP), WINDOW, axis=2)
    mix0, (wup_all, wdn_all, wo_a, wo_b) = swa_attention(
        q0_t, k0, v0_t, _slope_rows(slopes2, WINDOW), sink_rows, [w_up, w_down, w_o_a, w_o_b],
        batch=batch, seq=seq)
    x1 = oproj_norm_residual(mix0, xf, norm_attn_post[0], wo_a[0], tm=512, name="oproj0")
    x2 = mlp_block(x1, norm_mlp_pre[0], norm_mlp_post[0], wup_all, wdn_all, 0, tm=512, tf=1024, name="mlp0")

    q1_t, k1, v1_t, kmean = attn_proj(
        x2, norm_attn_pre[1], kv_norm, (w_q_b[0] * qscale).astype(BF16).T,
        wk_s.astype(BF16), wv_s.astype(BF16).T, key_feat,
        tm=512, seq=seq, shared_norm=False, with_kmean=True, name="attn_proj1")
    kmean = kmean.reshape(batch, seq // MOBA_BLOCK, K_AUG).astype(BF16)
    mix1 = moba_attention(q1_t, k1, v1_t, kmean, _slope_rows(slopes2, MOBA_BLOCK), batch=batch, seq=seq)
    x3 = oproj_norm_residual(mix1, x2, norm_attn_post[1], wo_b[0], tm=512, name="oproj1")
    x4 = mlp_block(x3, norm_mlp_pre[1], norm_mlp_post[1], wup_all, wdn_all, 1, tm=512, tf=1024, name="mlp1")
    return x4.reshape(batch, seq, d)
```

```python
import functools

import jax
import jax.numpy as jnp
import numpy as np
from jax import lax
from jax.experimental import pallas as pl
from jax.experimental.pallas import tpu as pltpu

D_MODEL = 2048
N_HEADS = 32
N_KV_HEADS = 4
HEAD_DIM = 64
GROUP = N_HEADS // N_KV_HEADS
PAIRS = GROUP // 2
PAIR_W = 2 * HEAD_DIM
Q_DIM = N_HEADS * HEAD_DIM
KV_DIM = N_KV_HEADS * HEAD_DIM
K_AUG = N_KV_HEADS * PAIR_W
D_FF = 4 * D_MODEL
WINDOW = 128
MOBA_BLOCK = 256
MOBA_TOPK = 3
RMS_EPS = 1e-6
NEG_INF = -1e30
BELOW_NEG_INF = -3e38
RUNNING_MAX_INIT = -1e38
LOG2E = 1.4426950408889634

SEL_LANES = 16
SLOPE_BLK_LANE = 16
SLOPE_POS_LANE = 19
SLOPE_PIECES = 3
SLOPE_ROWS = 8
ONES_ROWS = 16
NORM_ROW_CHUNK = 8

VMEM_LIMIT = 56 * 1024 * 1024

BF16 = jnp.bfloat16
F32 = jnp.float32
NT_DIMS = (((1,), (1,)), ((), ()))


def _rms_scale(x):
    return lax.rsqrt(jnp.mean(x * x, axis=-1, keepdims=True) + RMS_EPS)


def _cparams(semantics):
    return pltpu.CompilerParams(dimension_semantics=semantics,
                                vmem_limit_bytes=VMEM_LIMIT)


def _ones_rows():
    row = lax.broadcasted_iota(jnp.int32, (ONES_ROWS, MOBA_BLOCK), 0)
    return jnp.where(row == 0, 1.0, 0.0).astype(BF16)


PROJ_CHUNK = 512


def _attn_proj_kernel(x_ref, gq_ref, gkv_ref, wqt_ref, wk_ref, wvt_ref, feat_ref,
                      qt_ref, k_ref, vt_ref, *km_refs, shared_norm):
    x = x_ref[...]
    xn = x * _rms_scale(x)
    hq = (xn * gq_ref[...]).astype(BF16)
    hkv = hq if shared_norm else (xn * gkv_ref[...]).astype(BF16)
    nblk = x_ref.shape[0] // MOBA_BLOCK
    for c in range(0, wqt_ref.shape[1], PROJ_CHUNK):
        qt = lax.dot_general(wqt_ref[:, c:c + PROJ_CHUNK], hq, (((0,), (1,)), ((), ())),
                             preferred_element_type=F32).astype(BF16)
        for blk in range(nblk):
            qt_ref[blk, c:c + PROJ_CHUNK, :] = qt[:, blk * MOBA_BLOCK:(blk + 1) * MOBA_BLOCK]
    kacc = jnp.dot(hkv, wk_ref[...], preferred_element_type=F32)
    feat = feat_ref[...]
    lane = lax.broadcasted_iota(jnp.int32, feat.shape, 1)
    for kh in range(N_KV_HEADS):
        two_heads = kacc[:, (kh // 2) * PAIR_W:(kh // 2 + 1) * PAIR_W]
        if kh % 2:
            two_heads = jnp.concatenate([two_heads[:, HEAD_DIM:], two_heads[:, :HEAD_DIM]], axis=1)
        k_head = jnp.where(lane < HEAD_DIM, two_heads, 0.0)
        cols = slice(kh * PAIR_W, (kh + 1) * PAIR_W)
        for km_ref in km_refs:
            for blk in range(nblk):
                rows = k_head[blk * MOBA_BLOCK:(blk + 1) * MOBA_BLOCK]
                km_ref[blk, :, cols] = jnp.sum(rows, axis=0, keepdims=True) * (1.0 / MOBA_BLOCK)
        k_ref[:, cols] = (k_head + feat).astype(BF16)
    vt = lax.dot_general(wvt_ref[...], hkv, NT_DIMS, preferred_element_type=F32)
    for blk in range(nblk):
        vt_ref[blk] = vt[:, blk * MOBA_BLOCK:(blk + 1) * MOBA_BLOCK].astype(BF16)


def attn_proj(x, gq, gkv, wq_t, wk, wv_t, key_feat, *, tm, seq, shared_norm, with_kmean, name):
    n, d = x.shape
    nb = tm // MOBA_BLOCK
    tiles_per_seq = seq // tm
    whole = lambda w: pl.BlockSpec(w.shape, lambda i: (0, 0))
    gain = pl.BlockSpec((1, d), lambda i: (0, 0))
    out_specs = [pl.BlockSpec((nb, wq_t.shape[0], MOBA_BLOCK), lambda i: (i, 0, 0)),
                 pl.BlockSpec((tm, K_AUG), lambda i: (i, 0)),
                 pl.BlockSpec((nb, KV_DIM, MOBA_BLOCK), lambda i: (i, 0, 0))]
    out_shape = [jax.ShapeDtypeStruct((n // MOBA_BLOCK, wq_t.shape[0], MOBA_BLOCK), BF16),
                 jax.ShapeDtypeStruct((n, K_AUG), BF16),
                 jax.ShapeDtypeStruct((n // MOBA_BLOCK, KV_DIM, MOBA_BLOCK), BF16)]
    if with_kmean:
        out_specs.append(pl.BlockSpec((nb, 1, K_AUG), lambda i: (i, 0, 0)))
        out_shape.append(jax.ShapeDtypeStruct((n // MOBA_BLOCK, 1, K_AUG), F32))
    return pl.pallas_call(
        functools.partial(_attn_proj_kernel, shared_norm=shared_norm),
        grid=(n // tm,),
        in_specs=[pl.BlockSpec((tm, d), lambda i: (i, 0)), gain, gain,
                  whole(wq_t), whole(wk), whole(wv_t),
                  pl.BlockSpec((tm, PAIR_W), lambda i: (i % tiles_per_seq, 0))],
        out_specs=out_specs,
        out_shape=out_shape,
        compiler_params=_cparams(("parallel",)),
        name=name,
    )(x, gq.reshape(1, d), gkv.reshape(1, d), wq_t, wk, wv_t, key_feat)


def _oproj_kernel(mix_ref, x_ref, g_ref, w_ref, o_ref):
    a = jnp.dot(mix_ref[...], w_ref[...], preferred_element_type=F32)
    o_ref[...] = x_ref[...] + (a * _rms_scale(a)) * g_ref[...]


def oproj_norm_residual(mix, x, g, w, *, tm, name):
    n, d = x.shape
    return pl.pallas_call(
        _oproj_kernel,
        grid=(n // tm,),
        in_specs=[pl.BlockSpec((tm, mix.shape[1]), lambda i: (i, 0)),
                  pl.BlockSpec((tm, d), lambda i: (i, 0)),
                  pl.BlockSpec((1, d), lambda i: (0, 0)),
                  pl.BlockSpec(w.shape, lambda i: (0, 0))],
        out_specs=pl.BlockSpec((tm, d), lambda i: (i, 0)),
        out_shape=jax.ShapeDtypeStruct((n, d), F32),
        compiler_params=_cparams(("parallel",)),
        name=name,
    )(mix, x, g.reshape(1, d), w)


MLP_FF_CHUNK = 512


def _mlp_kernel(x_ref, gpre_ref, gpost_ref, wup_ref, wdn_ref, o_ref, h_ref, acc_ref):
    f = pl.program_id(1)

    @pl.when(f == 0)
    def _():
        x = x_ref[...]
        h_ref[...] = ((x * _rms_scale(x)) * gpre_ref[...]).astype(BF16)
        acc_ref[...] = jnp.zeros(acc_ref.shape, F32)

    h = h_ref[...]
    chunks = wup_ref.shape[1] // MLP_FF_CHUNK

    def up(c):
        cols = slice(c * MLP_FF_CHUNK, (c + 1) * MLP_FF_CHUNK)
        return jnp.dot(h, wup_ref[:, cols], preferred_element_type=F32)

    u_next = up(0)
    for c in range(chunks):
        u = jnp.maximum(u_next, 0.0)
        if c + 1 < chunks:
            u_next = up(c + 1)
        rows = slice(c * MLP_FF_CHUNK, (c + 1) * MLP_FF_CHUNK)
        acc_ref[...] += jnp.dot((u * u).astype(BF16), wdn_ref[rows, :], preferred_element_type=F32)

    @pl.when(f == pl.num_programs(1) - 1)
    def _():
        for r in range(0, acc_ref.shape[0], NORM_ROW_CHUNK):
            a = acc_ref[r:r + NORM_ROW_CHUNK, :]
            o_ref[r:r + NORM_ROW_CHUNK, :] = (
                x_ref[r:r + NORM_ROW_CHUNK, :] + (a * _rms_scale(a)) * gpost_ref[...])


def mlp_block(x, gpre, gpost, wup_all, wdn_all, layer, *, tm, tf, name):
    n, d = x.shape
    dff = wup_all.shape[2]
    return pl.pallas_call(
        _mlp_kernel,
        grid=(n // tm, dff // tf),
        in_specs=[pl.BlockSpec((tm, d), lambda i, f: (i, 0)),
                  pl.BlockSpec((1, d), lambda i, f: (0, 0)),
                  pl.BlockSpec((1, d), lambda i, f: (0, 0)),
                  pl.BlockSpec((None, d, tf), lambda i, f: (layer, 0, f)),
                  pl.BlockSpec((None, tf, d), lambda i, f: (layer, f, 0))],
        out_specs=pl.BlockSpec((tm, d), lambda i, f: (i, 0)),
        out_shape=jax.ShapeDtypeStruct((n, d), F32),
        scratch_shapes=[pltpu.VMEM((tm, d), BF16), pltpu.VMEM((tm, d), F32)],
        compiler_params=_cparams(("parallel", "arbitrary")),
        name=name,
    )(x, gpre.reshape(1, d), gpost.reshape(1, d), wup_all, wdn_all)


SWA_PV_LAG = 6
SWA_BLOCKS_PER_STEP = 2


def _swa_kernel(slope_ref, sink_ref, q_ref, k_ref, vt_ref, *refs, n_cast):
    o_ref, lhs_ref = refs[n_cast], refs[-1]
    for src_ref, dst_ref in zip(refs[:n_cast], refs[n_cast + 1:2 * n_cast + 1]):
        dst_ref[...] = src_ref[...].astype(BF16)

    half_lanes = GROUP * WINDOW
    zeros_top = jnp.zeros((SEL_LANES, half_lanes), F32)
    zeros_bottom = jnp.zeros((HEAD_DIM - SEL_LANES - SLOPE_ROWS, half_lanes), F32)
    ones_rows = _ones_rows()
    key = lax.broadcasted_iota(jnp.int32, (MOBA_BLOCK, 2 * WINDOW), 0)
    qry = lax.broadcasted_iota(jnp.int32, (MOBA_BLOCK, 2 * WINDOW), 1) & (WINDOW - 1)
    lane_q = lax.broadcasted_iota(jnp.int32, (1, half_lanes), 1) & (WINDOW - 1)
    slopes = [slope_ref[kh, 0:1, :] + slope_ref[kh, 1:2, :] + slope_ref[kh, 2:3, :]
              for kh in range(N_KV_HEADS)]

    base, v_half, keep, sinks = {}, {}, {}, {}
    for blk in range(SWA_BLOCKS_PER_STEP):
        j = pl.program_id(1) * SWA_BLOCKS_PER_STEP + blk
        for kh in range(N_KV_HEADS):
            feat = jnp.concatenate([zeros_top, slope_ref[kh], zeros_bottom], axis=0).astype(BF16)
            for half in range(2):
                for h in range(GROUP):
                    row0 = (kh * GROUP + h) * HEAD_DIM
                    lhs_ref[blk, kh, half, 0:HEAD_DIM, h * WINDOW:(h + 1) * WINDOW] = (
                        q_ref[blk, row0:row0 + HEAD_DIM, half * WINDOW:(half + 1) * WINDOW])
                lhs_ref[blk, kh, half, HEAD_DIM:2 * HEAD_DIM, :] = feat

        base[blk] = [jnp.maximum(j * MOBA_BLOCK - WINDOW, 0), j * MOBA_BLOCK]
        v_own = vt_ref[j]
        v_prev = vt_ref[jnp.maximum(j - 1, 0)]
        v_straddle = jnp.concatenate([v_prev[:, WINDOW:], v_own[:, :WINDOW]], axis=1)
        v_half[blk] = [jnp.where(j > 0, v_straddle, v_own), v_own]
        for half in range(2):
            dist = (j * MOBA_BLOCK + half * WINDOW - base[blk][half]) + qry - key
            keep[(blk, half)] = (dist & -WINDOW) == 0
            pos = (j * MOBA_BLOCK + half * WINDOW + lane_q).astype(F32)
            for kh in range(N_KV_HEADS):
                sinks[(blk, kh, half)] = sink_ref[kh] + slopes[kh] * pos

    def keys(blk, kh, half):
        start = pl.multiple_of(base[blk][half], WINDOW)
        return k_ref[pl.ds(start, MOBA_BLOCK), kh * PAIR_W:(kh + 1) * PAIR_W]

    def values(blk, kh, half):
        return jnp.concatenate([v_half[blk][half][kh * HEAD_DIM:(kh + 1) * HEAD_DIM, :], ones_rows], axis=0)

    items = [(blk, kh, half, c) for blk in range(SWA_BLOCKS_PER_STEP) for kh in range(N_KV_HEADS)
             for c in range(PAIRS) for half in range(2)]
    raw, pvs = {}, {}

    def pair_cols(c):
        return slice(c * 2 * WINDOW, (c + 1) * 2 * WINDOW)

    def scores(i):
        blk, kh, half, c = items[i]
        raw[i] = jnp.dot(keys(blk, kh, half), lhs_ref[blk, kh, half, :, pair_cols(c)],
                         preferred_element_type=F32)

    def softmax_pv(i):
        blk, kh, half, c = items[i]
        s = jnp.where(keep[(blk, half)], raw.pop(i), NEG_INF)
        sink = sinks[(blk, kh, half)][:, pair_cols(c)]
        m = jnp.maximum(jnp.max(s, axis=0, keepdims=True), sink)
        e = jnp.exp2(s - m).astype(BF16)
        pv = jnp.dot(values(blk, kh, half), e, preferred_element_type=F32)
        pvs[i] = (pv, jnp.exp2(sink - m))

    def normalise(i):
        pv, sink_term = pvs.pop(i)
        l = pv[HEAD_DIM:HEAD_DIM + 1, :] + sink_term
        return pv[0:HEAD_DIM, :] * (1.0 / l)

    done = {}
    for i in range(len(items) + SWA_PV_LAG + 1):
        if i < len(items):
            scores(i)
        if 0 <= i - SWA_PV_LAG < len(items):
            softmax_pv(i - SWA_PV_LAG)
        n = i - SWA_PV_LAG - 1
        if 0 <= n < len(items):
            blk, kh, half, c = items[n]
            done[half] = normalise(n)
            if half == 1:
                pair = jnp.concatenate(
                    [jnp.concatenate([done[0][:, :WINDOW], done[1][:, :WINDOW]], axis=1),
                     jnp.concatenate([done[0][:, WINDOW:], done[1][:, WINDOW:]], axis=1)],
                    axis=0)
                col0 = (kh * PAIRS + c) * PAIR_W
                o_ref[blk * MOBA_BLOCK:(blk + 1) * MOBA_BLOCK, col0:col0 + PAIR_W] = (
                    pair.T.astype(o_ref.dtype))


def swa_attention(q_t, k_aug, v_t, slope_rows, sink_rows, to_bf16, *, batch, seq):
    n = k_aug.shape[0]
    nblk = seq // MOBA_BLOCK
    half_lanes = GROUP * WINDOW
    per_seq = nblk // SWA_BLOCKS_PER_STEP
    steps = batch * per_seq
    cast_specs = [pl.BlockSpec((w.shape[0], w.shape[1] // steps, w.shape[2]),
                               lambda b, j: (0, b * per_seq + j, 0)) for w in to_bf16]
    outs = pl.pallas_call(
        functools.partial(_swa_kernel, n_cast=len(to_bf16)),
        grid=(batch, per_seq),
        in_specs=[pl.BlockSpec(slope_rows.shape, lambda b, j: (0, 0, 0)),
                  pl.BlockSpec(sink_rows.shape, lambda b, j: (0, 0, 0)),
                  pl.BlockSpec((SWA_BLOCKS_PER_STEP, Q_DIM, MOBA_BLOCK), lambda b, j: (b * per_seq + j, 0, 0)),
                  pl.BlockSpec((seq, K_AUG), lambda b, j: (b, 0)),
                  pl.BlockSpec((nblk, KV_DIM, MOBA_BLOCK), lambda b, j: (b, 0, 0))] + cast_specs,
        out_specs=[pl.BlockSpec((SWA_BLOCKS_PER_STEP * MOBA_BLOCK, Q_DIM),
                                lambda b, j: (b * per_seq + j, 0))] + cast_specs,
        out_shape=[jax.ShapeDtypeStruct((n, Q_DIM), BF16)]
                  + [jax.ShapeDtypeStruct(w.shape, BF16) for w in to_bf16],
        scratch_shapes=[pltpu.VMEM((SWA_BLOCKS_PER_STEP, N_KV_HEADS, 2, 2 * HEAD_DIM, half_lanes), BF16)],
        compiler_params=_cparams(("parallel", "arbitrary")),
        name="swa_attention",
    )(slope_rows, sink_rows, q_t, k_aug, v_t, *to_bf16)
    return outs[0], outs[1:]


MOBA_PV_LAG = 2


def _moba_kernel(slope_ref, q_ref, k_ref, vt_ref, km_ref, o_ref,
                 lhs_ref, acc_ref, m_ref, s0_ref, s1_ref):
    lanes = GROUP * MOBA_BLOCK
    nblk = km_ref.shape[1]
    km = km_ref[0][:, :HEAD_DIM]
    blk = lax.broadcasted_iota(jnp.int32, (nblk, lanes), 0)
    blkf = blk.astype(F32)
    pad = jnp.zeros((HEAD_DIM - SEL_LANES - SLOPE_ROWS, lanes), F32)

    def build_operand(j, carry):
        qt = q_ref[j]
        gate = jnp.concatenate(
            [jnp.dot(km, qt[h * HEAD_DIM:(h + 1) * HEAD_DIM, :], preferred_element_type=F32)
             for h in range(GROUP)], axis=1)
        g = jnp.where(blk < j, gate, NEG_INF)
        sel = jnp.zeros((nblk, lanes), F32)
        for _ in range(MOBA_TOPK):
            mx = jnp.max(g, axis=0, keepdims=True)
            idx = jnp.min(jnp.where(g == mx, blkf, float(nblk)), axis=0, keepdims=True)
            hit = blkf == idx
            sel = jnp.where(hit, 1.0, sel)
            g = jnp.where(hit, BELOW_NEG_INF, g)
        not_sel = jnp.where(blk < j, 1.0 - sel, jnp.where(blk == j, 0.0, 1.0))
        feat = jnp.concatenate([not_sel, slope_ref[0], pad], axis=0).astype(BF16)
        for h in range(GROUP):
            lhs_ref[j, h, 0:HEAD_DIM, :] = qt[h * HEAD_DIM:(h + 1) * HEAD_DIM, :]
            lhs_ref[j, h, HEAD_DIM:2 * HEAD_DIM, :] = feat[:, h * MOBA_BLOCK:(h + 1) * MOBA_BLOCK]
        return carry

    lax.fori_loop(0, nblk, build_operand, 0, unroll=4)
    m_ref[...] = jnp.full(m_ref.shape, RUNNING_MAX_INIT, F32)
    acc_ref[...] = jnp.zeros(acc_ref.shape, F32)

    ones_rows = _ones_rows()
    key = lax.broadcasted_iota(jnp.int32, (MOBA_BLOCK, MOBA_BLOCK), 0)
    qry = lax.broadcasted_iota(jnp.int32, (MOBA_BLOCK, MOBA_BLOCK), 1)

    def key_block(n):
        return k_ref[pl.ds(pl.multiple_of(n * MOBA_BLOCK, MOBA_BLOCK), MOBA_BLOCK), :]

    def values(n):
        return jnp.concatenate([vt_ref[n], ones_rows], axis=0)

    def softmax(src_ref, j, h):
        s = src_ref[h]
        m_old = m_ref[j, h]
        m_new = jnp.maximum(m_old, jnp.max(s, axis=0, keepdims=True))
        m_ref[j, h] = m_new
        return j, h, jnp.exp2(s - m_new).astype(BF16), jnp.exp2(m_old - m_new)

    def accumulate(v_aug, j, h, e, alpha):
        pv = jnp.dot(v_aug, e, preferred_element_type=F32)
        acc_ref[j, h] = acc_ref[j, h] * alpha + pv

    def step(j, t, src_ref, dst_ref, next_own):
        if next_own:
            jn = jnp.minimum(j + 1, nblk - 1)
            kblk = key_block(jn)
        else:
            jn = j
            kblk = key_block(t)
        v_aug = values(jnp.where(t == 0, j, t - 1))
        pending = []
        for h in range(GROUP):
            s_next = jnp.dot(kblk, lhs_ref[jn, h], preferred_element_type=F32)
            dst_ref[h] = jnp.where(key <= qry, s_next, NEG_INF) if next_own else s_next
            pending.append(softmax(src_ref, j, h))
            if len(pending) > MOBA_PV_LAG:
                accumulate(v_aug, *pending.pop(0))
        for ready in pending:
            accumulate(v_aug, *ready)

    def query_block(j, cur_ref, other_ref, odd):
        def two_steps(i, carry):
            step(j, 2 * i, cur_ref, other_ref, False)
            step(j, 2 * i + 1, other_ref, cur_ref, False)
            return carry

        def four_steps(i, carry):
            two_steps(2 * i, carry)
            return two_steps(2 * i + 1, carry)

        pairs = j // 2
        lax.fori_loop(0, pairs // 2, four_steps, 0)
        lax.fori_loop(2 * (pairs // 2), pairs, two_steps, 0)
        if odd:
            step(j, j - 1, cur_ref, other_ref, False)
            step(j, j, other_ref, cur_ref, True)
        else:
            step(j, j, cur_ref, other_ref, True)

    k_first = key_block(0)
    for h in range(GROUP):
        s0_ref[h] = jnp.where(key <= qry, jnp.dot(k_first, lhs_ref[0, h], preferred_element_type=F32),
                              NEG_INF)

    def four_query_blocks(i, carry):
        query_block(4 * i, s0_ref, s1_ref, False)
        query_block(4 * i + 1, s1_ref, s0_ref, True)
        query_block(4 * i + 2, s1_ref, s0_ref, False)
        query_block(4 * i + 3, s0_ref, s1_ref, True)
        return carry

    lax.fori_loop(0, nblk // 4, four_query_blocks, 0)

    def write_out(j, carry):
        rows = pl.ds(pl.multiple_of(j * MOBA_BLOCK, MOBA_BLOCK), MOBA_BLOCK)
        for p in range(PAIRS):
            pair = jnp.concatenate(
                [acc_ref[j, h, 0:HEAD_DIM, :] * (1.0 / acc_ref[j, h, HEAD_DIM:HEAD_DIM + 1, :])
                 for h in (2 * p, 2 * p + 1)], axis=0)
            o_ref[rows, p * PAIR_W:(p + 1) * PAIR_W] = pair.T.astype(o_ref.dtype)
        return carry

    lax.fori_loop(0, nblk, write_out, 0, unroll=2)


def moba_attention(q_t, k_aug, v_t, kmean, slope_rows, *, batch, seq):
    n = k_aug.shape[0]
    nblk = seq // MOBA_BLOCK
    assert nblk % 4 == 0
    lanes = GROUP * MOBA_BLOCK
    return pl.pallas_call(
        _moba_kernel,
        grid=(batch, N_KV_HEADS),
        in_specs=[pl.BlockSpec((1, SLOPE_ROWS, lanes), lambda b, kh: (kh, 0, 0)),
                  pl.BlockSpec((nblk, GROUP * HEAD_DIM, MOBA_BLOCK), lambda b, kh: (b, kh, 0)),
                  pl.BlockSpec((seq, PAIR_W), lambda b, kh: (b, kh)),
                  pl.BlockSpec((nblk, HEAD_DIM, MOBA_BLOCK), lambda b, kh: (b, kh, 0)),
                  pl.BlockSpec((1, nblk, PAIR_W), lambda b, kh: (b, 0, kh))],
        out_specs=pl.BlockSpec((seq, GROUP * HEAD_DIM), lambda b, kh: (b, kh)),
        out_shape=jax.ShapeDtypeStruct((n, Q_DIM), BF16),
        scratch_shapes=[pltpu.VMEM((nblk, GROUP, 2 * HEAD_DIM, MOBA_BLOCK), BF16),
                        pltpu.VMEM((nblk, GROUP, HEAD_DIM + ONES_ROWS, MOBA_BLOCK), F32),
                        pltpu.VMEM((nblk, GROUP, 1, MOBA_BLOCK), F32),
                        pltpu.VMEM((GROUP, MOBA_BLOCK, MOBA_BLOCK), F32),
                        pltpu.VMEM((GROUP, MOBA_BLOCK, MOBA_BLOCK), F32)],
        compiler_params=_cparams(("parallel", "arbitrary")),
        name="moba_attention",
    )(slope_rows, q_t, k_aug, v_t, kmean)


def _alibi_slopes_base2():
    slopes = np.exp2(-8.0 * np.arange(1, N_HEADS + 1, dtype=np.float32) / N_HEADS)
    return (slopes.astype(np.float32) * np.float32(LOG2E)).astype(np.float32)


def _slope_rows(slopes, width):
    hi = slopes.astype(BF16).astype(np.float32)
    mid = (slopes - hi).astype(BF16).astype(np.float32)
    lo = (slopes - hi - mid).astype(BF16).astype(np.float32)
    pieces = np.stack([hi, mid, lo], axis=-1)
    pad = np.zeros((N_HEADS, SLOPE_ROWS - 2 * SLOPE_PIECES), np.float32)
    rows = np.concatenate([pieces, pieces, pad], axis=1)
    rows = rows.reshape(N_KV_HEADS, GROUP, SLOPE_ROWS).transpose(0, 2, 1)
    return np.repeat(rows, width, axis=2)


def _key_features(seq):
    pos = np.arange(seq, dtype=np.int32)[:, None]
    blk = pos // MOBA_BLOCK
    lane = np.arange(PAIR_W, dtype=np.int32)[None, :] - HEAD_DIM
    feat = np.where(lane == blk, np.float32(NEG_INF), np.float32(0.0))
    feat = np.where((lane >= SLOPE_BLK_LANE) & (lane < SLOPE_POS_LANE),
                    (blk * MOBA_BLOCK).astype(np.float32), feat)
    feat = np.where((lane >= SLOPE_POS_LANE) & (lane < SLOPE_POS_LANE + SLOPE_PIECES),
                    (pos % MOBA_BLOCK).astype(np.float32), feat)
    return np.where(lane < 0, np.float32(0.0), feat).astype(np.float32)


def kernel(x, w_qkv_a, sinks_a, w_o_a, kv_norm, w_kv_shared, w_q_b, w_o_b,
           norm_attn_pre, norm_attn_post, norm_mlp_pre, norm_mlp_post, w_up, w_down):
    batch, seq, d = x.shape
    n = batch * seq
    assert seq // MOBA_BLOCK == SEL_LANES and seq % MOBA_BLOCK == 0
    qscale = HEAD_DIM ** -0.5 * LOG2E
    slopes2 = _alibi_slopes_base2()
    key_feat = _key_features(seq)

    wq_a, wk_a, wv_a = jnp.split(w_qkv_a[0], [Q_DIM, Q_DIM + KV_DIM], axis=-1)
    wk_s, wv_s = jnp.split(w_kv_shared, [KV_DIM], axis=-1)

    xf = x.reshape(n, d)

    q0_t, k0, v0_t = attn_proj(
        xf, norm_attn_pre[0], norm_attn_pre[0], (wq_a * qscale).astype(BF16),
        wk_a.astype(BF16), wv_a.astype(BF16).T, key_feat,
        tm=512, seq=seq, shared_norm=True, with_kmean=False, name="attn_proj0")
    sink_rows = jnp.repeat((sinks_a[0].astype(F32) * LOG2E).reshape(N_KV_HEADS, 1, GROUP), WINDOW, axis=2)
    mix0, (wup_all, wdn_all, wo_a, wo_b) = swa_attention(
        q0_t, k0, v0_t, _slope_rows(slopes2, WINDOW), sink_rows, [w_up, w_down, w_o_a, w_o_b],
        batch=batch, seq=seq)
    x1 = oproj_norm_residual(mix0, xf, norm_attn_post[0], wo_a[0], tm=512, name="oproj0")
    x2 = mlp_block(x1, norm_mlp_pre[0], norm_mlp_post[0], wup_all, wdn_all, 0, tm=512, tf=1024, name="mlp0")

    q1_t, k1, v1_t, kmean = attn_proj(
        x2, norm_attn_pre[1], kv_norm, (w_q_b[0] * qscale).astype(BF16),
        wk_s.astype(BF16), wv_s.astype(BF16).T, key_feat,
        tm=512, seq=seq, shared_norm=False, with_kmean=True, name="attn_proj1")
    kmean = kmean.reshape(batch, seq // MOBA_BLOCK, K_AUG).astype(BF16)
    mix1 = moba_attention(q1_t, k1, v1_t, kmean, _slope_rows(slopes2, MOBA_BLOCK), batch=batch, seq=seq)
    x3 = oproj_norm_residual(mix1, x2, norm_attn_post[1], wo_b[0], tm=512, name="oproj1")
    x4 = mlp_block(x3, norm_mlp_pre[1], norm_mlp_post[1], wup_all, wdn_all, 1, tm=512, tf=1024, name="mlp1")
    return x4.reshape(batch, seq, d)
```
